```python
import jax, jax.numpy as jnp
from jax import lax
import numpy as np

D_MODEL = 1024
BATCH = 4
SEQ = 8192
DEPTH = 1
DEC_BATCH = 128
DEC_SEQ = 4
PAST_LEN = 16384
PAGE_SIZE = 128

N_HEADS = 8
QK_NOPE = 64
QK_ROPE = 32
V_HEAD = 64
Q_RANK = 384
KV_RANK = 256
CONV_CH = 512
CONV_WIDTH = 31
D_ATT = N_HEADS * V_HEAD
D_MIX = D_ATT + CONV_CH
D_IN = Q_RANK + KV_RANK + QK_ROPE + 2 * CONV_CH
D_FF = 2816
PLE_DIM = 256
ROPE_BASE = 10000.0
EPS = 1e-6
Q_BLOCK = 128
SCALE = (QK_NOPE + QK_ROPE) ** -0.5
NEG_INF = -1e30

kernel_name = 'hymba_mla_conformer_macaron_ple_step'


def rmsnorm(x, g):
    xf = x.astype(jnp.float32)
    y = xf * lax.rsqrt(jnp.mean(xf * xf, axis=-1, keepdims=True) + EPS)
    return (y * g.astype(jnp.float32)).astype(x.dtype)


def layernorm(x, g, b):
    xf = x.astype(jnp.float32)
    mu = jnp.mean(xf, axis=-1, keepdims=True)
    var = jnp.mean(jnp.square(xf - mu), axis=-1, keepdims=True)
    y = (xf - mu) * lax.rsqrt(var + EPS)
    return (y * g.astype(jnp.float32) + b.astype(jnp.float32)).astype(x.dtype)


def rope(x, pos):
    half = QK_ROPE // 2
    inv = ROPE_BASE ** (-jnp.arange(half, dtype=jnp.float32) / half)
    ang = pos.astype(jnp.float32)[:, None] * inv[None, :]
    ang = ang.reshape(ang.shape[:1] + (1,) * (x.ndim - 3) + ang.shape[1:])
    cos, sin = jnp.cos(ang), jnp.sin(ang)
    x1 = x[..., :half].astype(jnp.float32)
    x2 = x[..., half:].astype(jnp.float32)
    return jnp.concatenate([x1 * cos - x2 * sin, x1 * sin + x2 * cos], axis=-1).astype(x.dtype)


def swiglu(x, wg, wu, wd):
    return (jax.nn.silu(x @ wg) * (x @ wu)) @ wd


def prompt_attention(q_nope, q_pe, ckv, kpe, w_uk, w_uv):
    B, T, H, _ = q_nope.shape
    k_nope = jnp.einsum('bsr,rhn->bshn', ckv, w_uk)
    v = jnp.einsum('bsr,rhv->bshv', ckv, w_uv)
    blk = Q_BLOCK if T % Q_BLOCK == 0 else T
    outs = []
    for i in range(T // blk):
        q0, q1 = i * blk, (i + 1) * blk
        s = (jnp.einsum('bthn,bshn->bhts', q_nope[:, q0:q1], k_nope[:, :q1])
             + jnp.einsum('bthp,bsp->bhts', q_pe[:, q0:q1], kpe[:, :q1]))
        s = s.astype(jnp.float32) * SCALE
        causal = jnp.arange(q0, q1)[:, None] >= jnp.arange(q1)[None, :]
        s = jnp.where(causal, s, NEG_INF)
        pr = jax.nn.softmax(s, axis=-1).astype(v.dtype)
        outs.append(jnp.einsum('bhts,bshv->bthv', pr, v[:, :q1]))
    return jnp.concatenate(outs, axis=1).reshape(B, T, H * V_HEAD)


def sample_attention(q_nope, q_pe, q_pos, ckv_all, kpe_all, w_uk, w_uv):
    B, T, H, _ = q_nope.shape
    q_abs = jnp.einsum('bthn,rhn->bthr', q_nope, w_uk)
    s = (jnp.einsum('bthr,bsr->bhts', q_abs, ckv_all)
         + jnp.einsum('bthp,bsp->bhts', q_pe, kpe_all))
    s = s.astype(jnp.float32) * SCALE
    k_pos = jnp.arange(ckv_all.shape[1])
    s = jnp.where(k_pos[None, :] <= q_pos[:, None], s, NEG_INF)
    pr = jax.nn.softmax(s, axis=-1).astype(ckv_all.dtype)
    o_lat = jnp.einsum('bhts,bsr->bthr', pr, ckv_all)
    return jnp.einsum('bthr,rhv->bthv', o_lat, w_uv).reshape(B, T, H * V_HEAD)


def token_mix(n, prm, q_pos, attend, conv_hist):
    z = n @ prm['w_in']
    o1, o2, o3 = Q_RANK, Q_RANK + KV_RANK, Q_RANK + KV_RANK + QK_ROPE
    cq, ckv_raw, kpe_raw, u = z[..., :o1], z[..., o1:o2], z[..., o2:o3], z[..., o3:]
    q = jnp.einsum('btr,rhd->bthd', rmsnorm(cq, prm['g_q']), prm['w_q_up'])
    q_nope = q[..., :QK_NOPE]
    q_pe = rope(q[..., QK_NOPE:], q_pos)
    ckv = rmsnorm(ckv_raw, prm['g_kv'])
    kpe = rope(kpe_raw, q_pos)
    att = attend(q_nope, q_pe, ckv, kpe)
    v = u[..., :CONV_CH] * jax.nn.sigmoid(u[..., CONV_CH:])
    v_ext = jnp.concatenate([conv_hist, v], axis=1)
    c = lax.conv_general_dilated(v_ext, prm['conv_w'][:, None, :], window_strides=(1,),
                                 padding='VALID', dimension_numbers=('NWC', 'WIO', 'NWC'),
                                 feature_group_count=CONV_CH) + prm['conv_b']
    c = jax.nn.silu(layernorm(c, prm['g_cn'], prm['b_cn']))
    out = jnp.concatenate([att, c], axis=-1) @ prm['w_o']
    return out, ckv, kpe, v_ext[:, -(CONV_WIDTH - 1):]


def layer(h, p, prm, q_pos, attend, conv_hist):
    h = h + 0.5 * swiglu(rmsnorm(h, prm['g_ff1']), prm['w1_gate'], prm['w1_up'], prm['w1_down'])
    mix, ckv, kpe, conv_state = token_mix(rmsnorm(h, prm['g_mix']), prm, q_pos, attend, conv_hist)
    h = h + mix
    h = h + 0.5 * swiglu(rmsnorm(h, prm['g_ff2']), prm['w2_gate'], prm['w2_up'], prm['w2_down'])
    gate = jax.nn.sigmoid(rmsnorm(h, prm['g_ple']) @ prm['w_pg'] + prm['b_pg'])
    h = h + gate * (p @ prm['w_pp'])
    return h, ckv, kpe, conv_state


def setup_inputs(seed: int = 0) -> dict:
    key = jax.random.key(seed)
    ks = iter(jax.random.split(key, 48))
    n_pages = PAST_LEN // PAGE_SIZE
    n_pool = (DEC_BATCH * n_pages * 5) // 4
    f32 = jnp.float32

    def w(shape, fan_in):
        return jax.random.normal(next(ks), shape, f32) * (fan_in ** -0.5)

    def gain(shape):
        return 1.0 + 0.05 * jax.random.normal(next(ks), shape, f32)

    def bias(shape):
        return 0.02 * jax.random.normal(next(ks), shape, f32)

    L = DEPTH
    d = {}
    d['x_prompt'] = jax.random.normal(next(ks), (BATCH, SEQ, D_MODEL), f32)
    d['x_sample'] = jax.random.normal(next(ks), (DEC_BATCH, DEC_SEQ, D_MODEL), f32)
    d['cache_ckv'] = jax.random.normal(next(ks), (L, n_pool, PAGE_SIZE, KV_RANK), f32)
    d['cache_kpe'] = jax.random.normal(next(ks), (L, n_pool, PAGE_SIZE, QK_ROPE), f32)
    d['state_conv'] = jax.random.normal(next(ks), (L, DEC_BATCH, CONV_WIDTH - 1, CONV_CH), f32)
    perm = jax.random.permutation(next(ks), n_pool)
    d['page_table'] = perm[:DEC_BATCH * n_pages].reshape(DEC_BATCH, n_pages).astype(jnp.int32)
    d['p_prompt'] = jax.random.normal(next(ks), (L, BATCH, SEQ, PLE_DIM), f32)
    d['p_sample'] = jax.random.normal(next(ks), (L, DEC_BATCH, DEC_SEQ, PLE_DIM), f32)
    d['g_ff1'] = gain((L, D_MODEL))
    d['w1_gate'] = w((L, D_MODEL, D_FF), D_MODEL)
    d['w1_up'] = w((L, D_MODEL, D_FF), D_MODEL)
    d['w1_down'] = w((L, D_FF, D_MODEL), D_FF)
    d['g_mix'] = gain((L, D_MODEL))
    d['w_in'] = w((L, D_MODEL, D_IN), D_MODEL)
    d['g_q'] = gain((L, Q_RANK))
    d['w_q_up'] = w((L, Q_RANK, N_HEADS, QK_NOPE + QK_ROPE), Q_RANK)
    d['g_kv'] = gain((L, KV_RANK))
    d['w_uk'] = w((L, KV_RANK, N_HEADS, QK_NOPE), KV_RANK)
    d['w_uv'] = w((L, KV_RANK, N_HEADS, V_HEAD), KV_RANK)
    d['conv_w'] = w((L, CONV_WIDTH, CONV_CH), CONV_WIDTH)
    d['conv_b'] = bias((L, CONV_CH))
    d['g_cn'] = gain((L, CONV_CH))
    d['b_cn'] = bias((L, CONV_CH))
    d['w_o'] = w((L, D_MIX, D_MODEL), D_MIX)
    d['g_ff2'] = gain((L, D_MODEL))
    d['w2_gate'] = w((L, D_MODEL, D_FF), D_MODEL)
    d['w2_up'] = w((L, D_MODEL, D_FF), D_MODEL)
    d['w2_down'] = w((L, D_FF, D_MODEL), D_FF)
    d['g_ple'] = gain((L, D_MODEL))
    d['w_pg'] = w((L, D_MODEL, D_MODEL), D_MODEL)
    d['b_pg'] = bias((L, D_MODEL))
    d['w_pp'] = w((L, PLE_DIM, D_MODEL), PLE_DIM)
    d['g_final'] = gain((D_MODEL,))
    return d


def reference(x_prompt, x_sample, cache_ckv, cache_kpe, state_conv, page_table, p_prompt, p_sample,
              g_ff1, w1_gate, w1_up, w1_down, g_mix, w_in, g_q, w_q_up, g_kv, w_uk, w_uv,
              conv_w, conv_b, g_cn, b_cn, w_o, g_ff2, w2_gate, w2_up, w2_down,
              g_ple, w_pg, b_pg, w_pp, g_final):
    b_pr, t_pr = x_prompt.shape[0], x_prompt.shape[1]
    b_dec, t_dec = x_sample.shape[0], x_sample.shape[1]
    past_len = page_table.shape[1] * cache_ckv.shape[2]
    pos_prompt = jnp.arange(t_pr)
    pos_sample = past_len + jnp.arange(t_dec)
    conv_zero = jnp.zeros((b_pr, CONV_WIDTH - 1, CONV_CH), x_prompt.dtype)
    h_p, h_s = x_prompt, x_sample
    ckv_p, kpe_p, conv_p, ckv_s, kpe_s, conv_s = [], [], [], [], [], []
    for l in range(DEPTH):
        prm = dict(g_ff1=g_ff1[l], w1_gate=w1_gate[l], w1_up=w1_up[l], w1_down=w1_down[l],
                   g_mix=g_mix[l], w_in=w_in[l], g_q=g_q[l], w_q_up=w_q_up[l], g_kv=g_kv[l],
                   conv_w=conv_w[l], conv_b=conv_b[l], g_cn=g_cn[l], b_cn=b_cn[l], w_o=w_o[l],
                   g_ff2=g_ff2[l], w2_gate=w2_gate[l], w2_up=w2_up[l], w2_down=w2_down[l],
                   g_ple=g_ple[l], w_pg=w_pg[l], b_pg=b_pg[l], w_pp=w_pp[l])
        wuk, wuv = w_uk[l], w_uv[l]

        def attend_prompt(qn, qp, ckv, kpe):
            return prompt_attention(qn, qp, ckv, kpe, wuk, wuv)

        past_ckv = cache_ckv[l][page_table].reshape(b_dec, past_len, KV_RANK)
        past_kpe = cache_kpe[l][page_table].reshape(b_dec, past_len, QK_ROPE)

        def attend_sample(qn, qp, ckv, kpe):
            ckv_all = jnp.concatenate([past_ckv, ckv], axis=1)
            kpe_all = jnp.concatenate([past_kpe, kpe], axis=1)
            return sample_attention(qn, qp, pos_sample, ckv_all, kpe_all, wuk, wuv)

        h_p, c1, k1, s1 = layer(h_p, p_prompt[l], prm, pos_prompt, attend_prompt, conv_zero)
        h_s, c2, k2, s2 = layer(h_s, p_sample[l], prm, pos_sample, attend_sample, state_conv[l])
        ckv_p.append(c1); kpe_p.append(k1); conv_p.append(s1)
        ckv_s.append(c2); kpe_s.append(k2); conv_s.append(s2)
    y_prompt = rmsnorm(h_p, g_final)
    y_sample = rmsnorm(h_s, g_final)
    return (y_prompt, y_sample, jnp.stack(ckv_p), jnp.stack(kpe_p), jnp.stack(conv_p),
            jnp.stack(ckv_s), jnp.stack(kpe_s), jnp.stack(conv_s))
```

```python
import functools

import jax
import jax.numpy as jnp
from jax import lax
from jax.experimental import pallas as pl
from jax.experimental.pallas import tpu as pltpu

F32 = jnp.float32
BF16 = jnp.bfloat16

N_HEADS = 8
QK_NOPE = 64
QK_ROPE = 32
V_HEAD = 64
Q_RANK = 384
KV_RANK = 256
CONV_CH = 512
CONV_WIDTH = 31
ROPE_BASE = 10000.0
EPS = 1e-6
SCALE = (QK_NOPE + QK_ROPE) ** -0.5
NEG_INF = -1e30

LANES = 128
HEAD_PAD = LANES
CONV_HALO = 32
VMEM_LIMIT = 56 * 1024 * 1024


def _rms(x, g):
    return x * lax.rsqrt(jnp.mean(x * x, axis=-1, keepdims=True) + EPS) * g


def _dot(a, b):
    return jnp.dot(a, b, preferred_element_type=F32)


def _dot_t(a, b):
    return lax.dot_general(a, b, (((1,), (1,)), ((), ())), preferred_element_type=F32)


def _resident(shape):
    nd = len(shape)
    return pl.BlockSpec(shape, lambda *_: (0,) * nd, pipeline_mode=pl.Buffered(1))


def _params(n_axes):
    return pltpu.CompilerParams(dimension_semantics=("arbitrary",) * n_axes,
                                vmem_limit_bytes=VMEM_LIMIT)


def _swiglu(n, wg_ref, wu_ref, wd_ref):
    gate = _dot(n, wg_ref[...])
    up = _dot(n, wu_ref[...])
    a = (gate * jax.nn.sigmoid(gate) * up).astype(BF16)
    return _dot(a, wd_ref[...])


def _ffn_body(x_ref, g_ref, wg_ref, wu_ref, wd_ref, o_ref):
    x = x_ref[...]
    n = _rms(x, g_ref[...]).astype(BF16)
    o_ref[...] = x + 0.5 * _swiglu(n, wg_ref, wu_ref, wd_ref)


def _ffn(x, g, wg, wu, wd, tm):
    n, d = x.shape
    row = pl.BlockSpec((tm, d), lambda i: (i, 0))
    return pl.pallas_call(
        _ffn_body,
        grid=(n // tm,),
        in_specs=[row, _resident(g.shape), _resident(wg.shape), _resident(wu.shape), _resident(wd.shape)],
        out_specs=row,
        out_shape=jax.ShapeDtypeStruct((n, d), F32),
        compiler_params=_params(1),
        name="ffn1",
    )(x, g, wg, wu, wd)


_O_CQ = 0
_O_CKV = _O_CQ + Q_RANK
_O_UA = _O_CKV + KV_RANK
_O_UB = _O_UA + CONV_CH
_O_KPE = _O_UB + CONV_CH
_O_KPES = _O_KPE + LANES
_W_IN_COLS = _O_KPES + LANES


def _inproj_common(h_ref, gmix_ref, win_ref, gq_ref, wq_ref, wqs_ref, gkv_ref, cos_ref, sin_ref,
                   ckv_ref, kpe_ref, u_ref):
    n = _rms(h_ref[...], gmix_ref[...]).astype(BF16)
    z = _dot(n, win_ref[...])
    cos = cos_ref[...]
    sin = sin_ref[...]
    cqn = _rms(z[:, _O_CQ:_O_CKV], gq_ref[...]).astype(BF16)
    q = _dot(cqn, wq_ref[...])
    qs = _dot(cqn, wqs_ref[...])
    ckv = _rms(z[:, _O_CKV:_O_UA], gkv_ref[...])
    kpe = z[:, _O_KPE:_O_KPES] * cos + z[:, _O_KPES:_W_IN_COLS] * sin
    ckv_ref[...] = ckv
    kpe_ref[...] = kpe[:, :QK_ROPE]
    u_ref[...] = z[:, _O_UA:_O_UB] * jax.nn.sigmoid(z[:, _O_UB:_O_KPE])
    q_heads = []
    for h in range(N_HEADS):
        hs = slice(h * HEAD_PAD, (h + 1) * HEAD_PAD)
        q_heads.append(q[:, hs] * cos + qs[:, hs] * sin)
    return q_heads, ckv, kpe


def _inproj_prompt_body(h_ref, gmix_ref, win_ref, gq_ref, wq_ref, wqs_ref, gkv_ref, cos_ref, sin_ref,
                        wuk_ref, wuv_ref,
                        ckv_ref, kpe_ref, u_ref, q_ref, k_ref, v_ref):
    q_heads, ckv, kpe = _inproj_common(h_ref, gmix_ref, win_ref, gq_ref, wq_ref, wqs_ref, gkv_ref,
                                       cos_ref, sin_ref, ckv_ref, kpe_ref, u_ref)
    ckvb = ckv.astype(BF16)
    kn = _dot(ckvb, wuk_ref[...])
    vv = _dot(ckvb, wuv_ref[...])
    for h in range(N_HEADS):
        hs = slice(h * HEAD_PAD, (h + 1) * HEAD_PAD)
        q_ref[0, h] = q_heads[h].astype(BF16)
        k_ref[0, h] = (kn[:, hs] + kpe).astype(BF16)
    for hp in range(N_HEADS // 2):
        v_ref[0, hp] = vv[:, hp * LANES:(hp + 1) * LANES].astype(BF16)


def _inproj_sample_body(h_ref, gmix_ref, win_ref, gq_ref, wq_ref, wqs_ref, gkv_ref, cos_ref, sin_ref,
                        wukt_ref,
                        ckv_ref, kpe_ref, u_ref, qabs_ref, qpe_ref):
    q_heads, _, _ = _inproj_common(h_ref, gmix_ref, win_ref, gq_ref, wq_ref, wqs_ref, gkv_ref,
                                   cos_ref, sin_ref, ckv_ref, kpe_ref, u_ref)
    for h in range(N_HEADS):
        qh = q_heads[h].astype(BF16)
        qpe_ref[h] = qh
        qabs_ref[h] = _dot(qh, wukt_ref[h]).astype(BF16)


def _inproj(h, prm, cos, sin, tm, prompt_bt=None):
    n, d = h.shape
    tab_tiles = cos.shape[0] // tm
    row = lambda w: pl.BlockSpec((tm, w), lambda i: (i, 0))
    tab = pl.BlockSpec((tm, LANES), lambda i: (i % tab_tiles, 0))
    common_in = [row(d), _resident(prm["g_mix"].shape), _resident(prm["w_in"].shape),
                 _resident(prm["g_q"].shape), _resident(prm["w_q"].shape), _resident(prm["w_qs"].shape),
                 _resident(prm["g_kv"].shape), tab, tab]
    common_args = (h, prm["g_mix"], prm["w_in"], prm["g_q"], prm["w_q"], prm["w_qs"], prm["g_kv"], cos, sin)
    common_out = [row(KV_RANK), row(QK_ROPE), row(CONV_CH)]
    common_shape = [jax.ShapeDtypeStruct((n, KV_RANK), F32), jax.ShapeDtypeStruct((n, QK_ROPE), F32),
                    jax.ShapeDtypeStruct((n, CONV_CH), F32)]
    if prompt_bt is not None:
        b, t = prompt_bt
        tpb = t // tm
        head = lambda nh: pl.BlockSpec((1, nh, tm, LANES), lambda i: (i // tpb, 0, i % tpb, 0))
        return pl.pallas_call(
            _inproj_prompt_body,
            grid=(n // tm,),
            in_specs=common_in + [_resident(prm["w_uk"].shape), _resident(prm["w_uv"].shape)],
            out_specs=common_out + [head(N_HEADS), head(N_HEADS), head(N_HEADS // 2)],
            out_shape=common_shape + [jax.ShapeDtypeStruct((b, N_HEADS, t, LANES), BF16),
                                      jax.ShapeDtypeStruct((b, N_HEADS, t, LANES), BF16),
                                      jax.ShapeDtypeStruct((b, N_HEADS // 2, t, LANES), BF16)],
            compiler_params=_params(1),
            name="inproj_prompt",
        )(*common_args, prm["w_uk"], prm["w_uv"])
    hrow = lambda w: pl.BlockSpec((N_HEADS, tm, w), lambda i: (0, i, 0))
    return pl.pallas_call(
        _inproj_sample_body,
        grid=(n // tm,),
        in_specs=common_in + [_resident(prm["w_ukt"].shape)],
        out_specs=common_out + [hrow(KV_RANK), hrow(LANES)],
        out_shape=common_shape + [jax.ShapeDtypeStruct((N_HEADS, n, KV_RANK), BF16),
                                  jax.ShapeDtypeStruct((N_HEADS, n, LANES), BF16)],
        compiler_params=_params(1),
        name="inproj_sample",
    )(*common_args, prm["w_ukt"])


def _conv_body(tt, v_ref, hist_ref, w_ref, b_ref, g_ref, beta_ref, c_ref, state_ref, ext_ref):
    ti = pl.program_id(1)
    nt = pl.num_programs(1)

    @pl.when(ti == 0)
    def _():
        ext_ref[:, 0:CONV_HALO, :] = hist_ref[...]

    @pl.when(ti > 0)
    def _():
        ext_ref[:, 0:CONV_HALO, :] = ext_ref[:, tt:tt + CONV_HALO, :]

    ext_ref[:, CONV_HALO:CONV_HALO + tt, :] = v_ref[...]
    off = CONV_HALO - (CONV_WIDTH - 1)
    acc = ext_ref[:, off:off + tt, :] * w_ref[0:1, :]
    for k in range(1, CONV_WIDTH):
        acc = acc + ext_ref[:, off + k:off + k + tt, :] * w_ref[k:k + 1, :]
    acc = acc + b_ref[...]
    mu = jnp.mean(acc, axis=-1, keepdims=True)
    cen = acc - mu
    var = jnp.mean(cen * cen, axis=-1, keepdims=True)
    y = cen * lax.rsqrt(var + EPS) * g_ref[...] + beta_ref[...]
    c_ref[...] = (y * jax.nn.sigmoid(y)).astype(BF16)

    @pl.when(ti == nt - 1)
    def _():
        state_ref[...] = ext_ref[:, tt + off:tt + CONV_HALO, :]


def _conv(v, hist, w, b, g, beta, nb, tt):
    bsz, t, ch = v.shape
    hist = jnp.pad(hist, ((0, 0), (CONV_HALO - hist.shape[1], 0), (0, 0)))
    return pl.pallas_call(
        functools.partial(_conv_body, tt),
        grid=(bsz // nb, t // tt),
        in_specs=[pl.BlockSpec((nb, tt, ch), lambda i, j: (i, j, 0)),
                  pl.BlockSpec((nb, CONV_HALO, ch), lambda i, j: (i, 0, 0)),
                  _resident(w.shape), _resident(b.shape), _resident(g.shape), _resident(beta.shape)],
        out_specs=[pl.BlockSpec((nb, tt, ch), lambda i, j: (i, j, 0)),
                   pl.BlockSpec((nb, CONV_WIDTH - 1, ch), lambda i, j: (i, 0, 0))],
        out_shape=[jax.ShapeDtypeStruct((bsz, t, ch), BF16),
                   jax.ShapeDtypeStruct((bsz, CONV_WIDTH - 1, ch), F32)],
        scratch_shapes=[pltpu.VMEM((nb, tt + CONV_HALO, ch), F32)],
        compiler_params=_params(2),
        name="conv",
    )(v, hist, w, b, g, beta)


def _flash_body(tq, q_ref, k_ref, v_ref, o_ref):
    qi = pl.program_id(2)
    row = lax.broadcasted_iota(jnp.int32, (tq, tq), 0)
    col = lax.broadcasted_iota(jnp.int32, (tq, tq), 1)
    outs = []
    for hh in range(2):
        q = q_ref[0, hh]

        def block(j, carry, masked):
            m_prev, l_prev, acc = carry
            start = pl.multiple_of(j * tq, tq)
            k = k_ref[0, hh, pl.ds(start, tq), :]
            v = v_ref[0, 0, pl.ds(start, tq), :]
            s = _dot_t(q, k) * SCALE
            if masked:
                s = jnp.where(row >= col, s, NEG_INF)
            m_new = jnp.maximum(m_prev, jnp.max(s, axis=-1, keepdims=True))
            alpha = jnp.exp(m_prev - m_new)
            p = jnp.exp(s - m_new)
            l_new = alpha * l_prev + jnp.sum(p, axis=-1, keepdims=True)
            acc = alpha * acc + _dot(p.astype(BF16), v)
            return m_new, l_new, acc

        init = (jnp.full((tq, 1), NEG_INF, F32), jnp.zeros((tq, 1), F32), jnp.zeros((tq, LANES), F32))
        carry = lax.fori_loop(0, qi, lambda j, c: block(j, c, False), init)
        _, l_fin, acc = block(qi, carry, True)
        outs.append(acc / l_fin)
    lane = lax.broadcasted_iota(jnp.int32, (tq, LANES), 1)
    o_ref[0] = jnp.where(lane < V_HEAD, outs[0], outs[1]).astype(BF16)


def _flash(q, k, v, tq):
    b, nh, t, _ = q.shape
    return pl.pallas_call(
        functools.partial(_flash_body, tq),
        grid=(b, nh // 2, t // tq),
        in_specs=[pl.BlockSpec((1, 2, tq, LANES), lambda bi, hp, qi: (bi, hp, qi, 0)),
                  pl.BlockSpec((1, 2, t, LANES), lambda bi, hp, qi: (bi, hp, 0, 0)),
                  pl.BlockSpec((1, 1, t, LANES), lambda bi, hp, qi: (bi, hp, 0, 0))],
        out_specs=pl.BlockSpec((1, tq, LANES), lambda bi, hp, qi: (bi, qi, hp)),
        out_shape=jax.ShapeDtypeStruct((b, t, nh * V_HEAD), BF16),
        compiler_params=_params(3),
        name="flash_prompt",
    )(q, k, v)


def _sattn_body(ch, t_new, pt_ref, q_ref, qpe_ref, cnew_ref, knew_ref, ckv_hbm, kpe_hbm, o_ref,
                cbuf, kbuf, sem, m_ref, l_ref, acc_ref):
    b = pl.program_id(0)
    c = pl.program_id(1)
    nc = pl.num_programs(1)
    total = pl.num_programs(0) * nc
    step = b * nc + c
    slot = step % 2
    page = cbuf.shape[1] // ch

    def page_copies(bb, cc, sl):
        for j in range(ch):
            pg = pt_ref[bb, cc * ch + j]
            dst = pl.ds(j * page, page)
            yield pltpu.make_async_copy(ckv_hbm.at[pg], cbuf.at[sl, dst, :], sem.at[0, sl])
            yield pltpu.make_async_copy(kpe_hbm.at[pg], kbuf.at[sl, dst, :], sem.at[1, sl])

    @pl.when(step == 0)
    def _():
        for cp in page_copies(0, 0, 0):
            cp.start()

    @pl.when(step + 1 < total)
    def _():
        nxt = step + 1
        for cp in page_copies(nxt // nc, nxt % nc, 1 - slot):
            cp.start()

    @pl.when(c == 0)
    def _():
        m_ref[...] = jnp.full(m_ref.shape, NEG_INF, F32)
        l_ref[...] = jnp.zeros(l_ref.shape, F32)
        acc_ref[...] = jnp.zeros(acc_ref.shape, F32)

    for cp in page_copies(b, c, slot):
        cp.wait()

    q = q_ref[0]
    qpe = qpe_ref[0]
    ck = cbuf[slot].astype(BF16)
    kp = kbuf[slot].astype(BF16)
    s = (_dot_t(q, ck) + _dot_t(qpe, kp)) * SCALE
    m_prev = m_ref[...]
    m_new = jnp.maximum(m_prev, jnp.max(s, axis=-1, keepdims=True))
    alpha = jnp.exp(m_prev - m_new)
    p = jnp.exp(s - m_new)
    l_ref[...] = alpha * l_ref[...] + jnp.sum(p, axis=-1, keepdims=True)
    acc_ref[...] = alpha * acc_ref[...] + _dot(p.astype(BF16), ck)
    m_ref[...] = m_new

    @pl.when(c == nc - 1)
    def _():
        qf = q.astype(F32)
        qpf = qpe.astype(F32)
        cn = cnew_ref[0].astype(BF16).astype(F32)
        kn = knew_ref[0].astype(BF16).astype(F32)
        t_row = lax.broadcasted_iota(jnp.int32, (q.shape[0], 1), 0) % t_new
        s_new = []
        for j in range(t_new):
            sj = (jnp.sum(qf * cn[j:j + 1, :], axis=-1, keepdims=True)
                  + jnp.sum(qpf * kn[j:j + 1, :], axis=-1, keepdims=True)) * SCALE
            s_new.append(jnp.where(t_row >= j, sj, NEG_INF))
        m_old = m_ref[...]
        m_fin = m_old
        for sj in s_new:
            m_fin = jnp.maximum(m_fin, sj)
        a_fin = jnp.exp(m_old - m_fin)
        l_fin = a_fin * l_ref[...]
        acc = a_fin * acc_ref[...]
        for j, sj in enumerate(s_new):
            pj = jnp.exp(sj - m_fin)
            l_fin = l_fin + pj
            acc = acc + pj.astype(BF16).astype(F32) * cn[j:j + 1, :]
        o_ref[0] = acc / l_fin


def _sattn(page_table, q, qpe, ckv_new, kpe_new, cache_ckv, cache_kpe, ch):
    bsz, rows, _ = q.shape
    n_pages = page_table.shape[1]
    page = cache_ckv.shape[1]
    t_new = ckv_new.shape[1]
    grid_spec = pltpu.PrefetchScalarGridSpec(
        num_scalar_prefetch=1,
        grid=(bsz, n_pages // ch),
        in_specs=[pl.BlockSpec((1, rows, KV_RANK), lambda b, c, pt: (b, 0, 0)),
                  pl.BlockSpec((1, rows, QK_ROPE), lambda b, c, pt: (b, 0, 0)),
                  pl.BlockSpec((1, t_new, KV_RANK), lambda b, c, pt: (b, 0, 0)),
                  pl.BlockSpec((1, t_new, QK_ROPE), lambda b, c, pt: (b, 0, 0)),
                  pl.BlockSpec(memory_space=pl.ANY),
                  pl.BlockSpec(memory_space=pl.ANY)],
        out_specs=pl.BlockSpec((1, rows, KV_RANK), lambda b, c, pt: (b, 0, 0)),
        scratch_shapes=[pltpu.VMEM((2, ch * page, KV_RANK), F32),
                        pltpu.VMEM((2, ch * page, QK_ROPE), F32),
                        pltpu.SemaphoreType.DMA((2, 2)),
                        pltpu.VMEM((rows, 1), F32),
                        pltpu.VMEM((rows, 1), F32),
                        pltpu.VMEM((rows, KV_RANK), F32)],
    )
    return pl.pallas_call(
        functools.partial(_sattn_body, ch, t_new),
        grid_spec=grid_spec,
        out_shape=jax.ShapeDtypeStruct((bsz, rows, KV_RANK), F32),
        compiler_params=_params(2),
        name="attn_sample",
    )(page_table, q, qpe, ckv_new, kpe_new, cache_ckv, cache_kpe)


def _uv_body(o_ref, w_ref, att_ref):
    acc = _dot(o_ref[0].astype(BF16), w_ref[0])
    for h in range(1, N_HEADS):
        acc = acc + _dot(o_ref[h].astype(BF16), w_ref[h])
    att_ref[...] = acc.astype(BF16)


def _uv(o_lat, w_uv_pad):
    _, n, _ = o_lat.shape
    return pl.pallas_call(
        _uv_body,
        grid=(1,),
        in_specs=[_resident(o_lat.shape), _resident(w_uv_pad.shape)],
        out_specs=pl.BlockSpec((n, N_HEADS * V_HEAD), lambda i: (0, 0)),
        out_shape=jax.ShapeDtypeStruct((n, N_HEADS * V_HEAD), BF16),
        compiler_params=_params(1),
        name="uv_sample",
    )(o_lat, w_uv_pad)


def _post_body(last_layer, h_ref, att_ref, c_ref, p_ref, woa_ref, woc_ref, g2_ref, wg_ref, wu_ref, wd_ref,
               gple_ref, wpg_ref, bpg_ref, wpp_ref, gfin_ref, y_ref):
    h = h_ref[...] + _dot(att_ref[...], woa_ref[...]) + _dot(c_ref[...], woc_ref[...])
    h = h + 0.5 * _swiglu(_rms(h, g2_ref[...]).astype(BF16), wg_ref, wu_ref, wd_ref)
    gate = jax.nn.sigmoid(_dot(_rms(h, gple_ref[...]).astype(BF16), wpg_ref[...]) + bpg_ref[...])
    h = h + gate * _dot(p_ref[...].astype(BF16), wpp_ref[...])
    y_ref[...] = _rms(h, gfin_ref[...]) if last_layer else h


def _post(h, att, c, p, prm, tm, last_layer):
    n, d = h.shape
    row = lambda w: pl.BlockSpec((tm, w), lambda i: (i, 0))
    names = ["w_o_att", "w_o_conv", "g_ff2", "w2_gate", "w2_up", "w2_down", "g_ple", "w_pg", "b_pg", "w_pp",
             "g_final"]
    return pl.pallas_call(
        functools.partial(_post_body, last_layer),
        grid=(n // tm,),
        in_specs=[row(d), row(att.shape[1]), row(c.shape[1]), row(p.shape[1])]
                 + [_resident(prm[k].shape) for k in names],
        out_specs=row(d),
        out_shape=jax.ShapeDtypeStruct((n, d), F32),
        compiler_params=_params(1),
        name="post",
    )(h, att, c, p, *[prm[k] for k in names])


def _rope_tables(pos):
    half = QK_ROPE // 2
    inv = ROPE_BASE ** (-jnp.arange(half, dtype=F32) / half)
    ang = pos.astype(F32)[:, None] * inv[None, :]
    cos, sin = jnp.cos(ang), jnp.sin(ang)
    n = pos.shape[0]
    cos_t = jnp.concatenate([cos, cos, jnp.ones((n, QK_NOPE), F32), jnp.zeros((n, HEAD_PAD - QK_ROPE - QK_NOPE), F32)],
                            axis=1)
    sin_t = jnp.concatenate([-sin, sin, jnp.zeros((n, HEAD_PAD - QK_ROPE), F32)], axis=1)
    return cos_t, sin_t


def _swap_halves(w):
    half = QK_ROPE // 2
    return jnp.concatenate([w[..., half:], w[..., :half]], axis=-1)


def _prep_layer(l, g_ff1, w1_gate, w1_up, w1_down, g_mix, w_in, g_q, w_q_up, g_kv, w_uk, w_uv, conv_w, conv_b,
                g_cn, b_cn, w_o, g_ff2, w2_gate, w2_up, w2_down, g_ple, w_pg, b_pg, w_pp, g_final):
    vec = lambda a: a.reshape(1, -1)
    d_model = w_in.shape[1]
    prm = dict(g_ff1=vec(g_ff1[l]), w1_gate=w1_gate[l].astype(BF16), w1_up=w1_up[l].astype(BF16),
               w1_down=w1_down[l].astype(BF16), g_mix=vec(g_mix[l]), g_q=vec(g_q[l]), g_kv=vec(g_kv[l]),
               conv_w=conv_w[l], conv_b=vec(conv_b[l]), g_cn=vec(g_cn[l]), b_cn=vec(b_cn[l]),
               g_ff2=vec(g_ff2[l]), w2_gate=w2_gate[l].astype(BF16), w2_up=w2_up[l].astype(BF16),
               w2_down=w2_down[l].astype(BF16), g_ple=vec(g_ple[l]), w_pg=w_pg[l].astype(BF16),
               b_pg=vec(b_pg[l]), w_pp=w_pp[l].astype(BF16), g_final=vec(g_final))
    wi = w_in[l]
    o1, o2, o3 = Q_RANK, Q_RANK + KV_RANK, Q_RANK + KV_RANK + QK_ROPE
    w_kpe = wi[:, o2:o3]
    lane_pad = jnp.zeros((d_model, LANES - QK_ROPE), F32)
    prm["w_in"] = jnp.concatenate(
        [wi[:, :o2], wi[:, o3:], w_kpe, lane_pad, _swap_halves(w_kpe), lane_pad], axis=1).astype(BF16)
    wq = w_q_up[l]
    wq_nope, wq_pe = wq[..., :QK_NOPE], wq[..., QK_NOPE:]
    zq = lambda w: jnp.zeros(wq.shape[:2] + (w,), F32)
    prm["w_q"] = jnp.concatenate([wq_pe, wq_nope, zq(HEAD_PAD - QK_ROPE - QK_NOPE)], axis=-1
                                 ).reshape(Q_RANK, N_HEADS * HEAD_PAD).astype(BF16)
    prm["w_qs"] = jnp.concatenate([_swap_halves(wq_pe), zq(HEAD_PAD - QK_ROPE)], axis=-1
                                  ).reshape(Q_RANK, N_HEADS * HEAD_PAD).astype(BF16)
    wuk = w_uk[l]
    zk = lambda w: jnp.zeros(wuk.shape[:2] + (w,), F32)
    wuk_pad = jnp.concatenate([zk(QK_ROPE), wuk, zk(HEAD_PAD - QK_ROPE - QK_NOPE)], axis=-1)
    prm["w_uk"] = wuk_pad.reshape(KV_RANK, N_HEADS * HEAD_PAD).astype(BF16)
    prm["w_ukt"] = jnp.transpose(wuk_pad, (1, 2, 0)).astype(BF16)
    wuv = w_uv[l]
    prm["w_uv"] = wuv.reshape(KV_RANK, N_HEADS * V_HEAD).astype(BF16)
    eye = jnp.eye(N_HEADS, dtype=F32)
    prm["w_uv_pad"] = jnp.einsum("rhv,hg->hrgv", wuv, eye).reshape(N_HEADS, KV_RANK, N_HEADS * V_HEAD).astype(BF16)
    d_att = N_HEADS * V_HEAD
    prm["w_o_att"] = w_o[l][:d_att].astype(BF16)
    prm["w_o_conv"] = w_o[l][d_att:].astype(BF16)
    return prm


TM_FFN = 512
TM_PROJ = 512
TM_POST = 256
TQ_FLASH = 512
TT_CONV = 512
SAMPLE_CONV_BATCH = 16
SAMPLE_PAGES_PER_STEP = 32


def kernel(x_prompt, x_sample, cache_ckv, cache_kpe, state_conv, page_table, p_prompt, p_sample, g_ff1, w1_gate,
           w1_up, w1_down, g_mix, w_in, g_q, w_q_up, g_kv, w_uk, w_uv, conv_w, conv_b, g_cn, b_cn, w_o, g_ff2,
           w2_gate, w2_up, w2_down, g_ple, w_pg, b_pg, w_pp, g_final):
    b_pr, t_pr, d_model = x_prompt.shape
    b_dec, t_dec, _ = x_sample.shape
    depth = w_in.shape[0]
    past_len = page_table.shape[1] * cache_ckv.shape[2]
    n_pr, n_dec = b_pr * t_pr, b_dec * t_dec
    cos_p, sin_p = _rope_tables(jnp.arange(t_pr))
    cos_s, sin_s = _rope_tables(past_len + jnp.arange(n_dec) % t_dec)

    h_p = x_prompt.reshape(n_pr, d_model)
    h_s = x_sample.reshape(n_dec, d_model)
    outs = [[] for _ in range(6)]
    for l in range(depth):
        prm = _prep_layer(l, g_ff1, w1_gate, w1_up, w1_down, g_mix, w_in, g_q, w_q_up, g_kv, w_uk, w_uv, conv_w,
                          conv_b, g_cn, b_cn, w_o, g_ff2, w2_gate, w2_up, w2_down, g_ple, w_pg, b_pg, w_pp,
                          g_final)
        h1 = _ffn(h_p, prm["g_ff1"], prm["w1_gate"], prm["w1_up"], prm["w1_down"], TM_FFN)
        ckv, kpe, u, q, k, v = _inproj(h1, prm, cos_p, sin_p, TM_PROJ, prompt_bt=(b_pr, t_pr))
        att = _flash(q, k, v, TQ_FLASH).reshape(n_pr, -1)
        conv_zero = jnp.zeros((b_pr, CONV_WIDTH - 1, CONV_CH), F32)
        c, conv_state = _conv(u.reshape(b_pr, t_pr, CONV_CH), conv_zero, prm["conv_w"], prm["conv_b"],
                              prm["g_cn"], prm["b_cn"], 1, TT_CONV)
        h_p = _post(h1, att, c.reshape(n_pr, CONV_CH), p_prompt[l].reshape(n_pr, -1), prm, TM_POST,
                    l == depth - 1)
        outs[0].append(ckv.reshape(b_pr, t_pr, KV_RANK))
        outs[1].append(kpe.reshape(b_pr, t_pr, QK_ROPE))
        outs[2].append(conv_state)

        h1 = _ffn(h_s, prm["g_ff1"], prm["w1_gate"], prm["w1_up"], prm["w1_down"], n_dec)
        ckv, kpe, u, q_abs, q_pe = _inproj(h1, prm, cos_s, sin_s, n_dec)
        rows = N_HEADS * t_dec
        per_req = lambda a: jnp.transpose(a.reshape(N_HEADS, b_dec, t_dec, a.shape[-1]), (1, 0, 2, 3)
                                          ).reshape(b_dec, rows, a.shape[-1])
        ckv_new = ckv.reshape(b_dec, t_dec, KV_RANK)
        kpe_new = kpe.reshape(b_dec, t_dec, QK_ROPE)
        o_lat = _sattn(page_table, per_req(q_abs), per_req(q_pe[..., :QK_ROPE]), ckv_new, kpe_new,
                       cache_ckv[l], cache_kpe[l], SAMPLE_PAGES_PER_STEP)
        o_lat = jnp.transpose(o_lat.reshape(b_dec, N_HEADS, t_dec, KV_RANK), (1, 0, 2, 3)
                              ).reshape(N_HEADS, n_dec, KV_RANK)
        att = _uv(o_lat, prm["w_uv_pad"])
        c, conv_state = _conv(u.reshape(b_dec, t_dec, CONV_CH), state_conv[l], prm["conv_w"], prm["conv_b"],
                              prm["g_cn"], prm["b_cn"], SAMPLE_CONV_BATCH, t_dec)
        h_s = _post(h1, att, c.reshape(n_dec, CONV_CH), p_sample[l].reshape(n_dec, -1), prm, TM_POST,
                    l == depth - 1)
        outs[3].append(ckv_new)
        outs[4].append(kpe_new)
        outs[5].append(conv_state)

    y_prompt = h_p.reshape(b_pr, t_pr, d_model)
    y_sample = h_s.reshape(b_dec, t_dec, d_model)
    return (y_prompt, y_sample, jnp.stack(outs[0]), jnp.stack(outs[1]), jnp.stack(outs[2]),
            jnp.stack(outs[3]), jnp.stack(outs[4]), jnp.stack(outs[5]))
```

```python
import functools

import jax
import jax.numpy as jnp
from jax import lax
from jax.experimental import pallas as pl
from jax.experimental.pallas import tpu as pltpu

F32 = jnp.float32
BF16 = jnp.bfloat16

N_HEADS = 8
QK_NOPE = 64
QK_ROPE = 32
V_HEAD = 64
Q_RANK = 384
KV_RANK = 256
CONV_CH = 512
CONV_WIDTH = 31
ROPE_BASE = 10000.0
EPS = 1e-6
SCALE = (QK_NOPE + QK_ROPE) ** -0.5
LOG2_E = 1.4426950408889634
Q_SCALE = SCALE * LOG2_E
NEG_INF = -1e30

LANES = 128
SUBLANES = 8
HEAD_PAD = LANES
CONV_HALO = 32
VMEM_LIMIT = 56 * 1024 * 1024


def _rms(x, g):
    return x * lax.rsqrt(jnp.mean(x * x, axis=-1, keepdims=True) + EPS) * g


def _dot(a, b):
    return jnp.dot(a, b, preferred_element_type=F32)


def _dot_t(a, b):
    return lax.dot_general(a, b, (((1,), (1,)), ((), ())), preferred_element_type=F32)


def _resident(shape):
    nd = len(shape)
    return pl.BlockSpec(shape, lambda *_: (0,) * nd, pipeline_mode=pl.Buffered(1))


def _params(n_axes, flags=None):
    return pltpu.CompilerParams(dimension_semantics=("arbitrary",) * n_axes,
                                vmem_limit_bytes=VMEM_LIMIT, flags=flags)


def _swiglu(n, wg_ref, wu_ref, wd_ref):
    gate = _dot(n, wg_ref[...])
    up = _dot(n, wu_ref[...])
    a = (gate * jax.nn.sigmoid(gate) * up).astype(BF16)
    return _dot(a, wd_ref[...])


def _ffn_body(x_ref, g_ref, wg_ref, wu_ref, wd_ref, o_ref):
    x = x_ref[...]
    n = _rms(x, g_ref[...]).astype(BF16)
    o_ref[...] = x + 0.5 * _swiglu(n, wg_ref, wu_ref, wd_ref)


def _ffn(x, g, wg, wu, wd, tm):
    n, d = x.shape
    row = pl.BlockSpec((tm, d), lambda i: (i, 0))
    return pl.pallas_call(
        _ffn_body,
        grid=(n // tm,),
        in_specs=[row, _resident(g.shape), _resident(wg.shape), _resident(wu.shape), _resident(wd.shape)],
        out_specs=row,
        out_shape=jax.ShapeDtypeStruct((n, d), F32),
        compiler_params=_params(1),
        name="ffn1",
    )(x, g, wg, wu, wd)


_O_CQ = 0
_O_CKV = _O_CQ + Q_RANK
_O_UA = _O_CKV + KV_RANK
_O_UB = _O_UA + CONV_CH
_O_KPE = _O_UB + CONV_CH
_O_KPES = _O_KPE + LANES
_W_IN_COLS = _O_KPES + LANES


def _inproj_common(h_ref, gmix_ref, win_ref, gq_ref, wq_ref, wqs_ref, gkv_ref, cos_ref, sin_ref,
                   ckv_ref, kpe_ref, u_ref):
    n = _rms(h_ref[...], gmix_ref[...]).astype(BF16)
    z = _dot(n, win_ref[...])
    cos = cos_ref[...]
    sin = sin_ref[...]
    cqn = _rms(z[:, _O_CQ:_O_CKV], gq_ref[...]).astype(BF16)
    q = _dot(cqn, wq_ref[...])
    qs = _dot(cqn, wqs_ref[...])
    ckv = _rms(z[:, _O_CKV:_O_UA], gkv_ref[...])
    kpe = z[:, _O_KPE:_O_KPES] * cos + z[:, _O_KPES:_W_IN_COLS] * sin
    ckv_ref[...] = ckv
    kpe_ref[...] = kpe[:, :QK_ROPE]
    u_ref[...] = z[:, _O_UA:_O_UB] * jax.nn.sigmoid(z[:, _O_UB:_O_KPE])
    q_heads = []
    for h in range(N_HEADS):
        hs = slice(h * HEAD_PAD, (h + 1) * HEAD_PAD)
        q_heads.append((q[:, hs] * cos + qs[:, hs] * sin) * Q_SCALE)
    return q_heads, ckv, kpe


def _inproj_prompt_body(h_ref, gmix_ref, win_ref, gq_ref, wq_ref, wqs_ref, gkv_ref, cos_ref, sin_ref,
                        wuk_ref, wuv_ref,
                        ckv_ref, kpe_ref, u_ref, q_ref, k_ref, v_ref):
    q_heads, ckv, kpe = _inproj_common(h_ref, gmix_ref, win_ref, gq_ref, wq_ref, wqs_ref, gkv_ref,
                                       cos_ref, sin_ref, ckv_ref, kpe_ref, u_ref)
    ckvb = ckv.astype(BF16)
    kn = _dot(ckvb, wuk_ref[...])
    vv = _dot(ckvb, wuv_ref[...])
    for h in range(N_HEADS):
        hs = slice(h * HEAD_PAD, (h + 1) * HEAD_PAD)
        q_ref[0, h] = q_heads[h].astype(BF16)
        k_ref[0, h] = (kn[:, hs] + kpe).astype(BF16)
    lane = lax.broadcasted_iota(jnp.int32, (vv.shape[0], LANES), 1)
    for hp in range(N_HEADS // 2):
        pair = vv[:, hp * LANES:(hp + 1) * LANES]
        v_ref[0, 2 * hp] = jnp.where(lane < V_HEAD, pair, (lane == V_HEAD).astype(F32)).astype(BF16)
        v_ref[0, 2 * hp + 1] = jnp.where(lane >= V_HEAD, pair, (lane == 0).astype(F32)).astype(BF16)


def _inproj_sample_body(h_ref, gmix_ref, win_ref, gq_ref, wq_ref, wqs_ref, gkv_ref, cos_ref, sin_ref,
                        wukt_ref,
                        ckv_ref, kpe_ref, u_ref, qabs_ref, qpe_ref):
    q_heads, _, _ = _inproj_common(h_ref, gmix_ref, win_ref, gq_ref, wq_ref, wqs_ref, gkv_ref,
                                   cos_ref, sin_ref, ckv_ref, kpe_ref, u_ref)
    for h in range(N_HEADS):
        qh = q_heads[h].astype(BF16)
        qpe_ref[h] = qh
        qabs_ref[h] = _dot(qh, wukt_ref[h]).astype(BF16)


def _inproj(h, prm, cos, sin, tm, prompt_bt=None):
    n, d = h.shape
    tab_tiles = cos.shape[0] // tm
    row = lambda w: pl.BlockSpec((tm, w), lambda i: (i, 0))
    tab = pl.BlockSpec((tm, LANES), lambda i: (i % tab_tiles, 0))
    common_in = [row(d), _resident(prm["g_mix"].shape), _resident(prm["w_in"].shape),
                 _resident(prm["g_q"].shape), _resident(prm["w_q"].shape), _resident(prm["w_qs"].shape),
                 _resident(prm["g_kv"].shape), tab, tab]
    common_args = (h, prm["g_mix"], prm["w_in"], prm["g_q"], prm["w_q"], prm["w_qs"], prm["g_kv"], cos, sin)
    common_out = [row(KV_RANK), row(QK_ROPE), row(CONV_CH)]
    common_shape = [jax.ShapeDtypeStruct((n, KV_RANK), F32), jax.ShapeDtypeStruct((n, QK_ROPE), F32),
                    jax.ShapeDtypeStruct((n, CONV_CH), F32)]
    if prompt_bt is not None:
        b, t = prompt_bt
        tpb = t // tm
        head = lambda nh: pl.BlockSpec((1, nh, tm, LANES), lambda i: (i // tpb, 0, i % tpb, 0))
        return pl.pallas_call(
            _inproj_prompt_body,
            grid=(n // tm,),
            in_specs=common_in + [_resident(prm["w_uk"].shape), _resident(prm["w_uv"].shape)],
            out_specs=common_out + [head(N_HEADS)] * 3,
            out_shape=common_shape + [jax.ShapeDtypeStruct((b, N_HEADS, t, LANES), BF16)] * 3,
            compiler_params=_params(1),
            name="inproj_prompt",
        )(*common_args, prm["w_uk"], prm["w_uv"])
    hrow = lambda w: pl.BlockSpec((N_HEADS, tm, w), lambda i: (0, i, 0))
    return pl.pallas_call(
        _inproj_sample_body,
        grid=(n // tm,),
        in_specs=common_in + [_resident(prm["w_ukt"].shape)],
        out_specs=common_out + [hrow(KV_RANK), hrow(LANES)],
        out_shape=common_shape + [jax.ShapeDtypeStruct((N_HEADS, n, KV_RANK), BF16),
                                  jax.ShapeDtypeStruct((N_HEADS, n, LANES), BF16)],
        compiler_params=_params(1),
        name="inproj_sample",
    )(*common_args, prm["w_ukt"])


def _conv_body(tt, v_ref, hist_ref, w_ref, b_ref, g_ref, beta_ref, c_ref, state_ref, ext_ref, rot_ref):
    ti = pl.program_id(1)
    nt = pl.num_programs(1)

    @pl.when(ti == 0)
    def _():
        ext_ref[:, 0:CONV_HALO, :] = hist_ref[...]

    @pl.when(ti > 0)
    def _():
        ext_ref[:, 0:CONV_HALO, :] = ext_ref[:, tt:tt + CONV_HALO, :]

    ext_ref[:, CONV_HALO:CONV_HALO + tt, :] = v_ref[...]
    off = CONV_HALO - (CONV_WIDTH - 1)
    span = rot_ref.shape[2]
    for r in range(1, SUBLANES):
        rot_ref[r - 1] = ext_ref[:, r:r + span, :]

    def tap(k):
        r = (off + k) % SUBLANES
        base = off + k - r
        win = ext_ref[:, base:base + tt, :] if r == 0 else rot_ref[r - 1, :, base:base + tt, :]
        return win * w_ref[k:k + 1, :]

    acc = tap(0)
    for k in range(1, CONV_WIDTH):
        acc = acc + tap(k)
    acc = acc + b_ref[...]
    mu = jnp.mean(acc, axis=-1, keepdims=True)
    cen = acc - mu
    var = jnp.mean(cen * cen, axis=-1, keepdims=True)
    y = cen * lax.rsqrt(var + EPS) * g_ref[...] + beta_ref[...]
    c_ref[...] = (y * jax.nn.sigmoid(y)).astype(BF16)

    @pl.when(ti == nt - 1)
    def _():
        state_ref[...] = ext_ref[:, tt + off:tt + CONV_HALO, :]


def _conv(v, hist, w, b, g, beta, nb, tt):
    bsz, t, ch = v.shape
    hist = jnp.pad(hist, ((0, 0), (CONV_HALO - hist.shape[1], 0), (0, 0)))
    return pl.pallas_call(
        functools.partial(_conv_body, tt),
        grid=(bsz // nb, t // tt),
        in_specs=[pl.BlockSpec((nb, tt, ch), lambda i, j: (i, j, 0)),
                  pl.BlockSpec((nb, CONV_HALO, ch), lambda i, j: (i, 0, 0)),
                  _resident(w.shape), _resident(b.shape), _resident(g.shape), _resident(beta.shape)],
        out_specs=[pl.BlockSpec((nb, tt, ch), lambda i, j: (i, j, 0)),
                   pl.BlockSpec((nb, CONV_WIDTH - 1, ch), lambda i, j: (i, 0, 0))],
        out_shape=[jax.ShapeDtypeStruct((bsz, t, ch), BF16),
                   jax.ShapeDtypeStruct((bsz, CONV_WIDTH - 1, ch), F32)],
        scratch_shapes=[pltpu.VMEM((nb, tt + CONV_HALO, ch), F32),
                        pltpu.VMEM((SUBLANES - 1, nb, tt + CONV_HALO - SUBLANES, ch), F32)],
        compiler_params=_params(2),
        name="conv",
    )(v, hist, w, b, g, beta)


def _flash_body(tq, unroll, q_ref, k_ref, v_ref, o_ref):
    qi = pl.program_id(2)
    qs = (q_ref[0, 0], q_ref[0, 1])

    def absorb(j, state, masked):
        start = pl.multiple_of(j * tq, tq)
        out = []
        for hh in range(2):
            m_prev, acc = state[hh]
            s = _dot_t(qs[hh], k_ref[0, hh, pl.ds(start, tq), :])
            if masked:
                row = lax.broadcasted_iota(jnp.int32, (tq, tq), 0)
                col = lax.broadcasted_iota(jnp.int32, (tq, tq), 1)
                s = jnp.where(row >= col, s, NEG_INF)
            m_new = jnp.maximum(m_prev, jnp.max(s, axis=-1, keepdims=True))
            p = jnp.exp2(s - m_new).astype(BF16)
            pv = _dot(p, v_ref[0, hh, pl.ds(start, tq), :])
            out.append((m_new, jnp.exp2(m_prev - m_new) * acc + pv))
        return tuple(out)

    def trip(g, state):
        for u in range(unroll):
            state = absorb(g * unroll + u, state, False)
        return state

    init_h = (jnp.full((tq, 1), NEG_INF, F32), jnp.zeros((tq, LANES), F32))
    state = lax.fori_loop(0, qi // unroll, trip, (init_h, init_h))
    u = unroll // 2
    while u >= 1:
        base = (qi // (2 * u)) * (2 * u)

        def leftover(st, base=base, u=u):
            for i in range(u):
                st = absorb(base + i, st, False)
            return st

        state = lax.cond((qi // u) % 2 == 1, leftover, lambda st: st, state)
        u //= 2
    (_, acc0), (_, acc1) = absorb(qi, state, True)
    lane = lax.broadcasted_iota(jnp.int32, (tq, LANES), 1)
    o_ref[0] = jnp.where(lane < V_HEAD, acc0 / acc0[:, V_HEAD:V_HEAD + 1], acc1 / acc1[:, 0:1]).astype(BF16)


def _flash(q, k, v, tq):
    b, nh, t, _ = q.shape
    return pl.pallas_call(
        functools.partial(_flash_body, tq, FLASH_UNROLL),
        grid=(b, nh // 2, t // tq),
        in_specs=[pl.BlockSpec((1, 2, tq, LANES), lambda bi, hp, qi: (bi, hp, qi, 0)),
                  pl.BlockSpec((1, 2, t, LANES), lambda bi, hp, qi: (bi, hp, 0, 0)),
                  pl.BlockSpec((1, 2, t, LANES), lambda bi, hp, qi: (bi, hp, 0, 0))],
        out_specs=pl.BlockSpec((1, tq, LANES), lambda bi, hp, qi: (bi, qi, hp)),
        out_shape=jax.ShapeDtypeStruct((b, t, nh * V_HEAD), BF16),
        compiler_params=_params(3),
        name="flash_prompt",
    )(q, k, v)


def _sattn_body(ch, t_new, pt_ref, q_ref, qpe_ref, cnew_ref, knew_ref, ckv_hbm, kpe_hbm, o_ref,
                cbuf, kbuf, sem, m_ref, l_ref, acc_ref):
    b = pl.program_id(0)
    c = pl.program_id(1)
    nc = pl.num_programs(1)
    total = pl.num_programs(0) * nc
    step = b * nc + c
    slot = step % 2
    page = cbuf.shape[1] // ch

    def page_copies(bb, cc, sl):
        for j in range(ch):
            pg = pt_ref[bb, cc * ch + j]
            dst = pl.ds(j * page, page)
            yield pltpu.make_async_copy(ckv_hbm.at[pg], cbuf.at[sl, dst, :], sem.at[0, sl])
            yield pltpu.make_async_copy(kpe_hbm.at[pg], kbuf.at[sl, j], sem.at[1, sl])

    @pl.when(step == 0)
    def _():
        for cp in page_copies(0, 0, 0):
            cp.start()

    @pl.when(step + 1 < total)
    def _():
        nxt = step + 1
        for cp in page_copies(nxt // nc, nxt % nc, 1 - slot):
            cp.start()

    @pl.when(c == 0)
    def _():
        m_ref[...] = jnp.full(m_ref.shape, NEG_INF, F32)
        l_ref[...] = jnp.zeros(l_ref.shape, F32)
        acc_ref[...] = jnp.zeros(acc_ref.shape, F32)

    for cp in page_copies(b, c, slot):
        cp.wait()

    q = q_ref[0]
    qpe = qpe_ref[0]
    ck = cbuf[slot].astype(BF16)
    s_pe = jnp.concatenate([_dot(qpe, kbuf[slot, j].astype(BF16)) for j in range(ch)], axis=1)
    s = _dot_t(q, ck) + s_pe
    m_prev = m_ref[...]
    m_new = jnp.maximum(m_prev, jnp.max(s, axis=-1, keepdims=True))
    alpha = jnp.exp2(m_prev - m_new)
    p = jnp.exp2(s - m_new)
    l_ref[...] = alpha * l_ref[...] + jnp.sum(p, axis=-1, keepdims=True)
    acc_ref[...] = alpha * acc_ref[...] + _dot(p.astype(BF16), ck)
    m_ref[...] = m_new

    @pl.when(c == nc - 1)
    def _():
        qf = q.astype(F32)
        qpf = qpe.astype(F32)
        cn = cnew_ref[0].astype(BF16).astype(F32)
        kn = knew_ref[0].astype(BF16).astype(F32)
        t_row = lax.broadcasted_iota(jnp.int32, (q.shape[0], 1), 0) % t_new
        s_new = []
        for j in range(t_new):
            sj = (jnp.sum(qf * cn[j:j + 1, :], axis=-1, keepdims=True)
                  + jnp.sum(qpf * kn[j:j + 1, :], axis=-1, keepdims=True))
            s_new.append(jnp.where(t_row >= j, sj, NEG_INF))
        m_old = m_ref[...]
        m_fin = m_old
        for sj in s_new:
            m_fin = jnp.maximum(m_fin, sj)
        a_fin = jnp.exp2(m_old - m_fin)
        l_fin = a_fin * l_ref[...]
        acc = a_fin * acc_ref[...]
        for j, sj in enumerate(s_new):
            pj = jnp.exp2(sj - m_fin)
            l_fin = l_fin + pj
            acc = acc + pj.astype(BF16).astype(F32) * cn[j:j + 1, :]
        o_ref[0] = acc / l_fin


def _sattn(page_table, q, qpe, ckv_new, kpe_new, cache_ckv, cache_kpe, ch):
    bsz, rows, _ = q.shape
    n_pages = page_table.shape[1]
    page = cache_ckv.shape[1]
    t_new = ckv_new.shape[1]
    grid_spec = pltpu.PrefetchScalarGridSpec(
        num_scalar_prefetch=1,
        grid=(bsz, n_pages // ch),
        in_specs=[pl.BlockSpec((1, rows, KV_RANK), lambda b, c, pt: (b, 0, 0)),
                  pl.BlockSpec((1, rows, QK_ROPE), lambda b, c, pt: (b, 0, 0)),
                  pl.BlockSpec((1, t_new, KV_RANK), lambda b, c, pt: (b, 0, 0)),
                  pl.BlockSpec((1, t_new, QK_ROPE), lambda b, c, pt: (b, 0, 0)),
                  pl.BlockSpec(memory_space=pl.ANY),
                  pl.BlockSpec(memory_space=pl.ANY)],
        out_specs=pl.BlockSpec((1, rows, KV_RANK), lambda b, c, pt: (b, 0, 0)),
        scratch_shapes=[pltpu.VMEM((2, ch * page, KV_RANK), F32),
                        pltpu.VMEM((2, ch, QK_ROPE, page), F32),
                        pltpu.SemaphoreType.DMA((2, 2)),
                        pltpu.VMEM((rows, 1), F32),
                        pltpu.VMEM((rows, 1), F32),
                        pltpu.VMEM((rows, KV_RANK), F32)],
    )
    return pl.pallas_call(
        functools.partial(_sattn_body, ch, t_new),
        grid_spec=grid_spec,
        out_shape=jax.ShapeDtypeStruct((bsz, rows, KV_RANK), F32),
        compiler_params=_params(2),
        name="attn_sample",
    )(page_table, q, qpe, ckv_new, kpe_new, cache_ckv, cache_kpe)


def _uv_body(o_ref, w_ref, att_ref):
    acc = _dot(o_ref[0].astype(BF16), w_ref[0])
    for h in range(1, N_HEADS):
        acc = acc + _dot(o_ref[h].astype(BF16), w_ref[h])
    att_ref[...] = acc.astype(BF16)


def _uv(o_lat, w_uv_pad):
    _, n, _ = o_lat.shape
    return pl.pallas_call(
        _uv_body,
        grid=(1,),
        in_specs=[_resident(o_lat.shape), _resident(w_uv_pad.shape)],
        out_specs=pl.BlockSpec((n, N_HEADS * V_HEAD), lambda i: (0, 0)),
        out_shape=jax.ShapeDtypeStruct((n, N_HEADS * V_HEAD), BF16),
        compiler_params=_params(1),
        name="uv_sample",
    )(o_lat, w_uv_pad)


def _post_body(last_layer, h_ref, att_ref, c_ref, p_ref, woa_ref, woc_ref, g2_ref, wg_ref, wu_ref, wd_ref,
               gple_ref, wpg_ref, bpg_ref, wpp_ref, gfin_ref, y_ref):
    h = h_ref[...] + _dot(att_ref[...], woa_ref[...]) + _dot(c_ref[...], woc_ref[...])
    h = h + 0.5 * _swiglu(_rms(h, g2_ref[...]).astype(BF16), wg_ref, wu_ref, wd_ref)
    gate = jax.nn.sigmoid(_dot(_rms(h, gple_ref[...]).astype(BF16), wpg_ref[...]) + bpg_ref[...])
    h = h + gate * _dot(p_ref[...].astype(BF16), wpp_ref[...])
    y_ref[...] = _rms(h, gfin_ref[...]) if last_layer else h


def _post(h, att, c, p, prm, tm, last_layer):
    n, d = h.shape
    row = lambda w: pl.BlockSpec((tm, w), lambda i: (i, 0))
    names = ["w_o_att", "w_o_conv", "g_ff2", "w2_gate", "w2_up", "w2_down", "g_ple", "w_pg", "b_pg", "w_pp",
             "g_final"]
    return pl.pallas_call(
        functools.partial(_post_body, last_layer),
        grid=(n // tm,),
        in_specs=[row(d), row(att.shape[1]), row(c.shape[1]), row(p.shape[1])]
                 + [_resident(prm[k].shape) for k in names],
        out_specs=row(d),
        out_shape=jax.ShapeDtypeStruct((n, d), F32),
        compiler_params=_params(1),
        name="post",
    )(h, att, c, p, *[prm[k] for k in names])


def _rope_tables(pos):
    half = QK_ROPE // 2
    inv = ROPE_BASE ** (-jnp.arange(half, dtype=F32) / half)
    ang = pos.astype(F32)[:, None] * inv[None, :]
    cos, sin = jnp.cos(ang), jnp.sin(ang)
    n = pos.shape[0]
    cos_t = jnp.concatenate([cos, cos, jnp.ones((n, QK_NOPE), F32), jnp.zeros((n, HEAD_PAD - QK_ROPE - QK_NOPE), F32)],
                            axis=1)
    sin_t = jnp.concatenate([-sin, sin, jnp.zeros((n, HEAD_PAD - QK_ROPE), F32)], axis=1)
    return cos_t, sin_t


def _swap_halves(w):
    half = QK_ROPE // 2
    return jnp.concatenate([w[..., half:], w[..., :half]], axis=-1)


def _prep_layer(l, g_ff1, w1_gate, w1_up, w1_down, g_mix, w_in, g_q, w_q_up, g_kv, w_uk, w_uv, conv_w, conv_b,
                g_cn, b_cn, w_o, g_ff2, w2_gate, w2_up, w2_down, g_ple, w_pg, b_pg, w_pp, g_final):
    vec = lambda a: a.reshape(1, -1)
    d_model = w_in.shape[1]
    prm = dict(g_ff1=vec(g_ff1[l]), w1_gate=w1_gate[l].astype(BF16), w1_up=w1_up[l].astype(BF16),
               w1_down=w1_down[l].astype(BF16), g_mix=vec(g_mix[l]), g_q=vec(g_q[l]), g_kv=vec(g_kv[l]),
               conv_w=conv_w[l], conv_b=vec(conv_b[l]), g_cn=vec(g_cn[l]), b_cn=vec(b_cn[l]),
               g_ff2=vec(g_ff2[l]), w2_gate=w2_gate[l].astype(BF16), w2_up=w2_up[l].astype(BF16),
               w2_down=w2_down[l].astype(BF16), g_ple=vec(g_ple[l]), w_pg=w_pg[l].astype(BF16),
               b_pg=vec(b_pg[l]), w_pp=w_pp[l].astype(BF16), g_final=vec(g_final))
    wi = w_in[l]
    o1, o2, o3 = Q_RANK, Q_RANK + KV_RANK, Q_RANK + KV_RANK + QK_ROPE
    w_kpe = wi[:, o2:o3]
    lane_pad = jnp.zeros((d_model, LANES - QK_ROPE), F32)
    prm["w_in"] = jnp.concatenate(
        [wi[:, :o2], wi[:, o3:], w_kpe, lane_pad, _swap_halves(w_kpe), lane_pad], axis=1).astype(BF16)
    wq = w_q_up[l]
    wq_nope, wq_pe = wq[..., :QK_NOPE], wq[..., QK_NOPE:]
    zq = lambda w: jnp.zeros(wq.shape[:2] + (w,), F32)
    prm["w_q"] = jnp.concatenate([wq_pe, wq_nope, zq(HEAD_PAD - QK_ROPE - QK_NOPE)], axis=-1
                                 ).reshape(Q_RANK, N_HEADS * HEAD_PAD).astype(BF16)
    prm["w_qs"] = jnp.concatenate([_swap_halves(wq_pe), zq(HEAD_PAD - QK_ROPE)], axis=-1
                                  ).reshape(Q_RANK, N_HEADS * HEAD_PAD).astype(BF16)
    wuk = w_uk[l]
    zk = lambda w: jnp.zeros(wuk.shape[:2] + (w,), F32)
    wuk_pad = jnp.concatenate([zk(QK_ROPE), wuk, zk(HEAD_PAD - QK_ROPE - QK_NOPE)], axis=-1)
    prm["w_uk"] = wuk_pad.reshape(KV_RANK, N_HEADS * HEAD_PAD).astype(BF16)
    prm["w_ukt"] = jnp.transpose(wuk_pad, (1, 2, 0)).astype(BF16)
    wuv = w_uv[l]
    prm["w_uv"] = wuv.reshape(KV_RANK, N_HEADS * V_HEAD).astype(BF16)
    eye = jnp.eye(N_HEADS, dtype=F32)
    prm["w_uv_pad"] = jnp.einsum("rhv,hg->hrgv", wuv, eye).reshape(N_HEADS, KV_RANK, N_HEADS * V_HEAD).astype(BF16)
    d_att = N_HEADS * V_HEAD
    prm["w_o_att"] = w_o[l][:d_att].astype(BF16)
    prm["w_o_conv"] = w_o[l][d_att:].astype(BF16)
    return prm


TM_FFN = 512
TM_PROJ = 512
TM_POST = 512
TQ_FLASH = 512
FLASH_UNROLL = 4
TT_CONV = 512
SAMPLE_CONV_BATCH = 16
SAMPLE_PAGES_PER_STEP = 128


def kernel(x_prompt, x_sample, cache_ckv, cache_kpe, state_conv, page_table, p_prompt, p_sample, g_ff1, w1_gate,
           w1_up, w1_down, g_mix, w_in, g_q, w_q_up, g_kv, w_uk, w_uv, conv_w, conv_b, g_cn, b_cn, w_o, g_ff2,
           w2_gate, w2_up, w2_down, g_ple, w_pg, b_pg, w_pp, g_final):
    b_pr, t_pr, d_model = x_prompt.shape
    b_dec, t_dec, _ = x_sample.shape
    depth = w_in.shape[0]
    past_len = page_table.shape[1] * cache_ckv.shape[2]
    n_pr, n_dec = b_pr * t_pr, b_dec * t_dec
    cos_p, sin_p = _rope_tables(jnp.arange(t_pr))
    cos_s, sin_s = _rope_tables(past_len + jnp.arange(n_dec) % t_dec)

    h_p = x_prompt.reshape(n_pr, d_model)
    h_s = x_sample.reshape(n_dec, d_model)
    outs = [[] for _ in range(6)]
    for l in range(depth):
        prm = _prep_layer(l, g_ff1, w1_gate, w1_up, w1_down, g_mix, w_in, g_q, w_q_up, g_kv, w_uk, w_uv, conv_w,
                          conv_b, g_cn, b_cn, w_o, g_ff2, w2_gate, w2_up, w2_down, g_ple, w_pg, b_pg, w_pp,
                          g_final)
        h1 = _ffn(h_p, prm["g_ff1"], prm["w1_gate"], prm["w1_up"], prm["w1_down"], TM_FFN)
        ckv, kpe, u, q, k, v = _inproj(h1, prm, cos_p, sin_p, TM_PROJ, prompt_bt=(b_pr, t_pr))
        att = _flash(q, k, v, TQ_FLASH).reshape(n_pr, -1)
        conv_zero = jnp.zeros((b_pr, CONV_WIDTH - 1, CONV_CH), F32)
        c, conv_state = _conv(u.reshape(b_pr, t_pr, CONV_CH), conv_zero, prm["conv_w"], prm["conv_b"],
                              prm["g_cn"], prm["b_cn"], 1, TT_CONV)
        h_p = _post(h1, att, c.reshape(n_pr, CONV_CH), p_prompt[l].reshape(n_pr, -1), prm, TM_POST,
                    l == depth - 1)
        outs[0].append(ckv.reshape(b_pr, t_pr, KV_RANK))
        outs[1].append(kpe.reshape(b_pr, t_pr, QK_ROPE))
        outs[2].append(conv_state)

        h1 = _ffn(h_s, prm["g_ff1"], prm["w1_gate"], prm["w1_up"], prm["w1_down"], n_dec)
        ckv, kpe, u, q_abs, q_pe = _inproj(h1, prm, cos_s, sin_s, n_dec)
        rows = N_HEADS * t_dec
        per_req = lambda a: jnp.transpose(a.reshape(N_HEADS, b_dec, t_dec, a.shape[-1]), (1, 0, 2, 3)
                                          ).reshape(b_dec, rows, a.shape[-1])
        ckv_new = ckv.reshape(b_dec, t_dec, KV_RANK)
        kpe_new = kpe.reshape(b_dec, t_dec, QK_ROPE)
        o_lat = _sattn(page_table, per_req(q_abs), per_req(q_pe[..., :QK_ROPE]), ckv_new, kpe_new,
                       cache_ckv[l], jnp.swapaxes(cache_kpe[l], 1, 2), SAMPLE_PAGES_PER_STEP)
        o_lat = jnp.transpose(o_lat.reshape(b_dec, N_HEADS, t_dec, KV_RANK), (1, 0, 2, 3)
                              ).reshape(N_HEADS, n_dec, KV_RANK)
        att = _uv(o_lat, prm["w_uv_pad"])
        c, conv_state = _conv(u.reshape(b_dec, t_dec, CONV_CH), state_conv[l], prm["conv_w"], prm["conv_b"],
                              prm["g_cn"], prm["b_cn"], SAMPLE_CONV_BATCH, t_dec)
        h_s = _post(h1, att, c.reshape(n_dec, CONV_CH), p_sample[l].reshape(n_dec, -1), prm, TM_POST,
                    l == depth - 1)
        outs[3].append(ckv_new)
        outs[4].append(kpe_new)
        outs[5].append(conv_state)

    y_prompt = h_p.reshape(b_pr, t_pr, d_model)
    y_sample = h_s.reshape(b_dec, t_dec, d_model)
    return (y_prompt, y_sample, jnp.stack(outs[0]), jnp.stack(outs[1]), jnp.stack(outs[2]),
            jnp.stack(outs[3]), jnp.stack(outs[4]), jnp.stack(outs[5]))
```

```python
import functools

import jax
import jax.numpy as jnp
from jax import lax
from jax.experimental import pallas as pl
from jax.experimental.pallas import tpu as pltpu

F32 = jnp.float32
BF16 = jnp.bfloat16

N_HEADS = 8
QK_NOPE = 64
QK_ROPE = 32
V_HEAD = 64
Q_RANK = 384
KV_RANK = 256
CONV_CH = 512
CONV_WIDTH = 31
ROPE_BASE = 10000.0
EPS = 1e-6
SCALE = (QK_NOPE + QK_ROPE) ** -0.5
LOG2_E = 1.4426950408889634
Q_SCALE = SCALE * LOG2_E
NEG_INF = -1e30

LANES = 128
SUBLANES = 8
HEAD_PAD = LANES
CONV_HALO = 32
VMEM_LIMIT = 56 * 1024 * 1024


def _rms(x, g):
    return x * lax.rsqrt(jnp.mean(x * x, axis=-1, keepdims=True) + EPS) * g


def _dot(a, b):
    return jnp.dot(a, b, preferred_element_type=F32)


def _dot_t(a, b):
    return lax.dot_general(a, b, (((1,), (1,)), ((), ())), preferred_element_type=F32)


def _resident(shape):
    nd = len(shape)
    return pl.BlockSpec(shape, lambda *_: (0,) * nd, pipeline_mode=pl.Buffered(1))


def _params(n_axes, flags=None):
    return pltpu.CompilerParams(dimension_semantics=("arbitrary",) * n_axes,
                                vmem_limit_bytes=VMEM_LIMIT, flags=flags)


def _swiglu(n, wg_ref, wu_ref, wd_ref):
    gate = _dot(n, wg_ref[...])
    up = _dot(n, wu_ref[...])
    a = (gate * jax.nn.sigmoid(gate) * up).astype(BF16)
    return _dot(a, wd_ref[...])


def _ffn_body(x_ref, g_ref, wg_ref, wu_ref, wd_ref, o_ref):
    x = x_ref[...]
    n = _rms(x, g_ref[...]).astype(BF16)
    o_ref[...] = x + 0.5 * _swiglu(n, wg_ref, wu_ref, wd_ref)


def _ffn(x, g, wg, wu, wd, tm):
    n, d = x.shape
    row = pl.BlockSpec((tm, d), lambda i: (i, 0))
    return pl.pallas_call(
        _ffn_body,
        grid=(n // tm,),
        in_specs=[row, _resident(g.shape), _resident(wg.shape), _resident(wu.shape), _resident(wd.shape)],
        out_specs=row,
        out_shape=jax.ShapeDtypeStruct((n, d), F32),
        compiler_params=_params(1),
        name="ffn1",
    )(x, g, wg, wu, wd)


_O_CQ = 0
_O_CKV = _O_CQ + Q_RANK
_O_UA = _O_CKV + KV_RANK
_O_UB = _O_UA + CONV_CH
_O_KPE = _O_UB + CONV_CH
_O_KPES = _O_KPE + LANES
_W_IN_COLS = _O_KPES + LANES


def _inproj_common(h_ref, gmix_ref, win_ref, gq_ref, wq_ref, wqs_ref, gkv_ref, cos_ref, sin_ref,
                   ckv_ref, kpe_ref):
    n = _rms(h_ref[...], gmix_ref[...]).astype(BF16)
    z = _dot(n, win_ref[...])
    cos = cos_ref[...]
    sin = sin_ref[...]
    cqn = _rms(z[:, _O_CQ:_O_CKV], gq_ref[...]).astype(BF16)
    q = _dot(cqn, wq_ref[...])
    qs = _dot(cqn, wqs_ref[...])
    ckv = _rms(z[:, _O_CKV:_O_UA], gkv_ref[...])
    kpe = z[:, _O_KPE:_O_KPES] * cos + z[:, _O_KPES:_W_IN_COLS] * sin
    ckv_ref[...] = ckv
    kpe_ref[...] = kpe[:, :QK_ROPE]
    u = z[:, _O_UA:_O_UB] * jax.nn.sigmoid(z[:, _O_UB:_O_KPE])
    q_heads = []
    for h in range(N_HEADS):
        hs = slice(h * HEAD_PAD, (h + 1) * HEAD_PAD)
        q_heads.append((q[:, hs] * cos + qs[:, hs] * sin) * Q_SCALE)
    return q_heads, ckv, kpe, u


def _inproj_prompt_body(tpb, h_ref, gmix_ref, win_ref, gq_ref, wq_ref, wqs_ref, gkv_ref, cos_ref, sin_ref,
                        wuk_ref, wuv_ref, hist_ref, cw_ref, cb_ref, gcn_ref, bcn_ref,
                        ckv_ref, kpe_ref, c_ref, state_ref, q_ref, k_ref, v_ref, ext_ref, rot_ref):
    _conv_init(pl.program_id(0) == 0, h_ref.shape[0], ext_ref)
    q_heads, ckv, kpe, u = _inproj_common(h_ref, gmix_ref, win_ref, gq_ref, wq_ref, wqs_ref, gkv_ref,
                                          cos_ref, sin_ref, ckv_ref, kpe_ref)
    _conv_tile(u.shape[0], pl.program_id(0) % tpb == 0, u[None], hist_ref, cw_ref, cb_ref, gcn_ref, bcn_ref,
               c_ref, state_ref, ext_ref, rot_ref)
    ckvb = ckv.astype(BF16)
    kn = _dot(ckvb, wuk_ref[...])
    vv = _dot(ckvb, wuv_ref[...])
    for h in range(N_HEADS):
        hs = slice(h * HEAD_PAD, (h + 1) * HEAD_PAD)
        q_ref[0, h] = q_heads[h].astype(BF16)
        k_ref[0, h] = (kn[:, hs] + kpe).astype(BF16)
    lane = lax.broadcasted_iota(jnp.int32, (vv.shape[0], LANES), 1)
    for hp in range(N_HEADS // 2):
        pair = vv[:, hp * LANES:(hp + 1) * LANES]
        v_ref[0, 2 * hp] = jnp.where(lane < V_HEAD, pair, (lane == V_HEAD).astype(F32)).astype(BF16)
        v_ref[0, 2 * hp + 1] = jnp.where(lane >= V_HEAD, pair, (lane == 0).astype(F32)).astype(BF16)


def _inproj_sample_body(h_ref, gmix_ref, win_ref, gq_ref, wq_ref, wqs_ref, gkv_ref, cos_ref, sin_ref,
                        wukt_ref,
                        ckv_ref, kpe_ref, u_ref, qabs_ref, qpe_ref):
    q_heads, _, _, u = _inproj_common(h_ref, gmix_ref, win_ref, gq_ref, wq_ref, wqs_ref, gkv_ref,
                                      cos_ref, sin_ref, ckv_ref, kpe_ref)
    u_ref[...] = u
    for h in range(N_HEADS):
        qh = q_heads[h].astype(BF16)
        qpe_ref[h] = qh
        qabs_ref[h] = _dot(qh, wukt_ref[h]).astype(BF16)


def _inproj(h, prm, cos, sin, tm, prompt_bt=None, hist=None):
    n, d = h.shape
    tab_tiles = cos.shape[0] // tm
    row = lambda w: pl.BlockSpec((tm, w), lambda i: (i, 0))
    tab = pl.BlockSpec((tm, LANES), lambda i: (i % tab_tiles, 0))
    common_in = [row(d), _resident(prm["g_mix"].shape), _resident(prm["w_in"].shape),
                 _resident(prm["g_q"].shape), _resident(prm["w_q"].shape), _resident(prm["w_qs"].shape),
                 _resident(prm["g_kv"].shape), tab, tab]
    common_args = (h, prm["g_mix"], prm["w_in"], prm["g_q"], prm["w_q"], prm["w_qs"], prm["g_kv"], cos, sin)
    common_out = [row(KV_RANK), row(QK_ROPE)]
    common_shape = [jax.ShapeDtypeStruct((n, KV_RANK), F32), jax.ShapeDtypeStruct((n, QK_ROPE), F32)]
    if prompt_bt is not None:
        b, t = prompt_bt
        tpb = t // tm
        head = lambda nh: pl.BlockSpec((1, nh, tm, LANES), lambda i: (i // tpb, 0, i % tpb, 0))
        conv_names = ["conv_w", "conv_b", "g_cn", "b_cn"]
        per_seq = lambda rows: pl.BlockSpec((1, rows, CONV_CH), lambda i: (i // tpb, 0, 0))
        return pl.pallas_call(
            functools.partial(_inproj_prompt_body, tpb),
            grid=(n // tm,),
            in_specs=common_in + [_resident(prm["w_uk"].shape), _resident(prm["w_uv"].shape), per_seq(CONV_HALO)]
                     + [_resident(prm[k].shape) for k in conv_names],
            out_specs=common_out + [row(CONV_CH), per_seq(CONV_WIDTH - 1)] + [head(N_HEADS)] * 3,
            out_shape=common_shape + [jax.ShapeDtypeStruct((n, CONV_CH), BF16),
                                      jax.ShapeDtypeStruct((b, CONV_WIDTH - 1, CONV_CH), F32)]
                      + [jax.ShapeDtypeStruct((b, N_HEADS, t, LANES), BF16)] * 3,
            scratch_shapes=_conv_scratch(1, tm, CONV_CH),
            compiler_params=_params(1),
            name="inproj_prompt",
        )(*common_args, prm["w_uk"], prm["w_uv"], _pad_hist(hist), *[prm[k] for k in conv_names])
    hrow = lambda w: pl.BlockSpec((N_HEADS, tm, w), lambda i: (0, i, 0))
    return pl.pallas_call(
        _inproj_sample_body,
        grid=(n // tm,),
        in_specs=common_in + [_resident(prm["w_ukt"].shape)],
        out_specs=common_out + [row(CONV_CH), hrow(KV_RANK), hrow(LANES)],
        out_shape=common_shape + [jax.ShapeDtypeStruct((n, CONV_CH), F32),
                                  jax.ShapeDtypeStruct((N_HEADS, n, KV_RANK), BF16),
                                  jax.ShapeDtypeStruct((N_HEADS, n, LANES), BF16)],
        compiler_params=_params(1),
        name="inproj_sample",
    )(*common_args, prm["w_ukt"])


def _conv_init(first_step, tt, ext_ref):
    @pl.when(first_step)
    def _():
        ext_ref[:, tt:tt + CONV_HALO, :] = jnp.zeros((ext_ref.shape[0], CONV_HALO, ext_ref.shape[2]), F32)


def _conv_tile(tt, first, v, hist_ref, w_ref, b_ref, g_ref, beta_ref, c_ref, state_ref, ext_ref, rot_ref):
    ext_ref[:, 0:CONV_HALO, :] = jnp.where(first, hist_ref[...], ext_ref[:, tt:tt + CONV_HALO, :])
    ext_ref[:, CONV_HALO:CONV_HALO + tt, :] = v
    off = CONV_HALO - (CONV_WIDTH - 1)
    span = rot_ref.shape[2]
    for r in range(1, SUBLANES):
        rot_ref[r - 1] = ext_ref[:, r:r + span, :]

    def tap(k):
        r = (off + k) % SUBLANES
        base = off + k - r
        win = ext_ref[:, base:base + tt, :] if r == 0 else rot_ref[r - 1, :, base:base + tt, :]
        return win * w_ref[k:k + 1, :]

    acc = tap(0)
    for k in range(1, CONV_WIDTH):
        acc = acc + tap(k)
    acc = acc + b_ref[...]
    mu = jnp.mean(acc, axis=-1, keepdims=True)
    cen = acc - mu
    var = jnp.mean(cen * cen, axis=-1, keepdims=True)
    y = cen * lax.rsqrt(var + EPS) * g_ref[...] + beta_ref[...]
    c_ref[...] = (y * jax.nn.sigmoid(y)).astype(BF16).reshape(c_ref.shape)
    state_ref[...] = ext_ref[:, tt + off:tt + CONV_HALO, :]


def _conv_scratch(nb, tt, ch):
    return [pltpu.VMEM((nb, tt + CONV_HALO, ch), F32),
            pltpu.VMEM((SUBLANES - 1, nb, tt + CONV_HALO - SUBLANES, ch), F32)]


def _pad_hist(hist):
    return jnp.pad(hist, ((0, 0), (CONV_HALO - hist.shape[1], 0), (0, 0)))


def _conv_body(tt, v_ref, hist_ref, w_ref, b_ref, g_ref, beta_ref, c_ref, state_ref, ext_ref, rot_ref):
    ti = pl.program_id(1)
    _conv_init(jnp.logical_and(pl.program_id(0) == 0, ti == 0), tt, ext_ref)
    _conv_tile(tt, ti == 0, v_ref[...], hist_ref, w_ref, b_ref, g_ref, beta_ref, c_ref, state_ref, ext_ref, rot_ref)


def _conv(v, hist, w, b, g, beta, nb, tt):
    bsz, t, ch = v.shape
    hist = _pad_hist(hist)
    return pl.pallas_call(
        functools.partial(_conv_body, tt),
        grid=(bsz // nb, t // tt),
        in_specs=[pl.BlockSpec((nb, tt, ch), lambda i, j: (i, j, 0)),
                  pl.BlockSpec((nb, CONV_HALO, ch), lambda i, j: (i, 0, 0)),
                  _resident(w.shape), _resident(b.shape), _resident(g.shape), _resident(beta.shape)],
        out_specs=[pl.BlockSpec((nb, tt, ch), lambda i, j: (i, j, 0)),
                   pl.BlockSpec((nb, CONV_WIDTH - 1, ch), lambda i, j: (i, 0, 0))],
        out_shape=[jax.ShapeDtypeStruct((bsz, t, ch), BF16),
                   jax.ShapeDtypeStruct((bsz, CONV_WIDTH - 1, ch), F32)],
        scratch_shapes=_conv_scratch(nb, tt, ch),
        compiler_params=_params(2),
        name="conv",
    )(v, hist, w, b, g, beta)


def _flash_body(tq, unroll, q_ref, k_ref, v_ref, o_ref):
    qi = pl.program_id(2)
    qs = (q_ref[0, 0], q_ref[0, 1])

    def absorb(j, state, masked):
        start = pl.multiple_of(j * tq, tq)
        out = []
        for hh in range(2):
            m_prev, acc = state[hh]
            s = _dot_t(qs[hh], k_ref[0, hh, pl.ds(start, tq), :])
            if masked:
                row = lax.broadcasted_iota(jnp.int32, (tq, tq), 0)
                col = lax.broadcasted_iota(jnp.int32, (tq, tq), 1)
                s = jnp.where(row >= col, s, NEG_INF)
            m_new = jnp.maximum(m_prev, jnp.max(s, axis=-1, keepdims=True))
            p = jnp.exp2(s - m_new).astype(BF16)
            pv = _dot(p, v_ref[0, hh, pl.ds(start, tq), :])
            out.append((m_new, jnp.exp2(m_prev - m_new) * acc + pv))
        return tuple(out)

    def trip(g, state):
        for u in range(unroll):
            state = absorb(g * unroll + u, state, False)
        return state

    init_h = (jnp.full((tq, 1), NEG_INF, F32), jnp.zeros((tq, LANES), F32))
    state = lax.fori_loop(0, qi // unroll, trip, (init_h, init_h))
    u = unroll // 2
    while u >= 1:
        base = (qi // (2 * u)) * (2 * u)

        def leftover(st, base=base, u=u):
            for i in range(u):
                st = absorb(base + i, st, False)
            return st

        state = lax.cond((qi // u) % 2 == 1, leftover, lambda st: st, state)
        u //= 2
    (_, acc0), (_, acc1) = absorb(qi, state, True)
    lane = lax.broadcasted_iota(jnp.int32, (tq, LANES), 1)
    o_ref[0] = jnp.where(lane < V_HEAD, acc0 / acc0[:, V_HEAD:V_HEAD + 1], acc1 / acc1[:, 0:1]).astype(BF16)


def _flash(q, k, v, tq):
    b, nh, t, _ = q.shape
    return pl.pallas_call(
        functools.partial(_flash_body, tq, FLASH_UNROLL),
        grid=(b, nh // 2, t // tq),
        in_specs=[pl.BlockSpec((1, 2, tq, LANES), lambda bi, hp, qi: (bi, hp, qi, 0)),
                  pl.BlockSpec((1, 2, t, LANES), lambda bi, hp, qi: (bi, hp, 0, 0)),
                  pl.BlockSpec((1, 2, t, LANES), lambda bi, hp, qi: (bi, hp, 0, 0))],
        out_specs=pl.BlockSpec((1, tq, LANES), lambda bi, hp, qi: (bi, qi, hp)),
        out_shape=jax.ShapeDtypeStruct((b, t, nh * V_HEAD), BF16),
        compiler_params=_params(3),
        name="flash_prompt",
    )(q, k, v)


def _sattn_body(ch, sub, t_new, pt_ref, q_ref, qpe_ref, cnew_ref, knew_ref, ckv_hbm, kpe_hbm, o_ref,
                cbuf, kbuf, sem, m_ref, l_ref, acc_ref):
    b = pl.program_id(0)
    c = pl.program_id(1)
    nc = pl.num_programs(1)
    total = pl.num_programs(0) * nc
    step = b * nc + c
    slot = step % 2
    page = cbuf.shape[1] // ch

    def page_copies(bb, cc, sl):
        for j in range(ch):
            pg = pt_ref[bb, cc * ch + j]
            dst = pl.ds(j * page, page)
            yield pltpu.make_async_copy(ckv_hbm.at[pg], cbuf.at[sl, dst, :], sem.at[0, sl])
            yield pltpu.make_async_copy(kpe_hbm.at[pg], kbuf.at[sl, j], sem.at[1, sl])

    @pl.when(step == 0)
    def _():
        for cp in page_copies(0, 0, 0):
            cp.start()

    @pl.when(step + 1 < total)
    def _():
        nxt = step + 1
        for cp in page_copies(nxt // nc, nxt % nc, 1 - slot):
            cp.start()

    @pl.when(c == 0)
    def _():
        m_ref[...] = jnp.full(m_ref.shape, NEG_INF, F32)
        l_ref[...] = jnp.zeros(l_ref.shape, F32)
        acc_ref[...] = jnp.zeros(acc_ref.shape, F32)

    for cp in page_copies(b, c, slot):
        cp.wait()

    q = q_ref[0]
    qpe = qpe_ref[0]
    m_run, l_run, acc = m_ref[...], l_ref[...], acc_ref[...]
    for g in range(ch // sub):
        ck = cbuf[slot, g * sub * page:(g + 1) * sub * page, :].astype(BF16)
        s_pe = jnp.concatenate([_dot(qpe, kbuf[slot, g * sub + j].astype(BF16)) for j in range(sub)], axis=1)
        s = _dot_t(q, ck) + s_pe
        m_new = jnp.maximum(m_run, jnp.max(s, axis=-1, keepdims=True))
        alpha = jnp.exp2(m_run - m_new)
        p = jnp.exp2(s - m_new)
        l_run = alpha * l_run + jnp.sum(p, axis=-1, keepdims=True)
        acc = alpha * acc + _dot(p.astype(BF16), ck)
        m_run = m_new
    l_ref[...] = l_run
    acc_ref[...] = acc
    m_ref[...] = m_run

    @pl.when(c == nc - 1)
    def _():
        qf = q.astype(F32)
        qpf = qpe.astype(F32)
        cn = cnew_ref[0].astype(BF16).astype(F32)
        kn = knew_ref[0].astype(BF16).astype(F32)
        t_row = lax.broadcasted_iota(jnp.int32, (q.shape[0], 1), 0) % t_new
        s_new = []
        for j in range(t_new):
            sj = (jnp.sum(qf * cn[j:j + 1, :], axis=-1, keepdims=True)
                  + jnp.sum(qpf * kn[j:j + 1, :], axis=-1, keepdims=True))
            s_new.append(jnp.where(t_row >= j, sj, NEG_INF))
        m_old = m_ref[...]
        m_fin = m_old
        for sj in s_new:
            m_fin = jnp.maximum(m_fin, sj)
        a_fin = jnp.exp2(m_old - m_fin)
        l_fin = a_fin * l_ref[...]
        acc = a_fin * acc_ref[...]
        for j, sj in enumerate(s_new):
            pj = jnp.exp2(sj - m_fin)
            l_fin = l_fin + pj
            acc = acc + pj.astype(BF16).astype(F32) * cn[j:j + 1, :]
        o_ref[0] = acc / l_fin


def _sattn(page_table, q, qpe, ckv_new, kpe_new, cache_ckv, cache_kpe, ch):
    bsz, rows, _ = q.shape
    n_pages = page_table.shape[1]
    page = cache_ckv.shape[1]
    t_new = ckv_new.shape[1]
    grid_spec = pltpu.PrefetchScalarGridSpec(
        num_scalar_prefetch=1,
        grid=(bsz, n_pages // ch),
        in_specs=[pl.BlockSpec((1, rows, KV_RANK), lambda b, c, pt: (b, 0, 0)),
                  pl.BlockSpec((1, rows, QK_ROPE), lambda b, c, pt: (b, 0, 0)),
                  pl.BlockSpec((1, t_new, KV_RANK), lambda b, c, pt: (b, 0, 0)),
                  pl.BlockSpec((1, t_new, QK_ROPE), lambda b, c, pt: (b, 0, 0)),
                  pl.BlockSpec(memory_space=pl.ANY),
                  pl.BlockSpec(memory_space=pl.ANY)],
        out_specs=pl.BlockSpec((1, rows, KV_RANK), lambda b, c, pt: (b, 0, 0)),
        scratch_shapes=[pltpu.VMEM((2, ch * page, KV_RANK), F32),
                        pltpu.VMEM((2, ch, QK_ROPE, page), F32),
                        pltpu.SemaphoreType.DMA((2, 2)),
                        pltpu.VMEM((rows, 1), F32),
                        pltpu.VMEM((rows, 1), F32),
                        pltpu.VMEM((rows, KV_RANK), F32)],
    )
    return pl.pallas_call(
        functools.partial(_sattn_body, ch, SAMPLE_PAGES_PER_SOFTMAX, t_new),
        grid_spec=grid_spec,
        out_shape=jax.ShapeDtypeStruct((bsz, rows, KV_RANK), F32),
        compiler_params=_params(2),
        name="attn_sample",
    )(page_table, q, qpe, ckv_new, kpe_new, cache_ckv, cache_kpe)


def _uv_body(o_ref, w_ref, att_ref):
    acc = _dot(o_ref[0].astype(BF16), w_ref[0])
    for h in range(1, N_HEADS):
        acc = acc + _dot(o_ref[h].astype(BF16), w_ref[h])
    att_ref[...] = acc.astype(BF16)


def _uv(o_lat, w_uv_pad):
    _, n, _ = o_lat.shape
    return pl.pallas_call(
        _uv_body,
        grid=(1,),
        in_specs=[_resident(o_lat.shape), _resident(w_uv_pad.shape)],
        out_specs=pl.BlockSpec((n, N_HEADS * V_HEAD), lambda i: (0, 0)),
        out_shape=jax.ShapeDtypeStruct((n, N_HEADS * V_HEAD), BF16),
        compiler_params=_params(1),
        name="uv_sample",
    )(o_lat, w_uv_pad)


def _post_body(last_layer, h_ref, att_ref, c_ref, p_ref, woa_ref, woc_ref, g2_ref, wg_ref, wu_ref, wd_ref,
               gple_ref, wpg_ref, bpg_ref, wpp_ref, gfin_ref, y_ref):
    h = h_ref[...] + _dot(att_ref[...], woa_ref[...]) + _dot(c_ref[...], woc_ref[...])
    h = h + 0.5 * _swiglu(_rms(h, g2_ref[...]).astype(BF16), wg_ref, wu_ref, wd_ref)
    gate = jax.nn.sigmoid(_dot(_rms(h, gple_ref[...]).astype(BF16), wpg_ref[...]) + bpg_ref[...])
    h = h + gate * _dot(p_ref[...].astype(BF16), wpp_ref[...])
    y_ref[...] = _rms(h, gfin_ref[...]) if last_layer else h


def _post(h, att, c, p, prm, tm, last_layer):
    n, d = h.shape
    row = lambda w: pl.BlockSpec((tm, w), lambda i: (i, 0))
    names = ["w_o_att", "w_o_conv", "g_ff2", "w2_gate", "w2_up", "w2_down", "g_ple", "w_pg", "b_pg", "w_pp",
             "g_final"]
    return pl.pallas_call(
        functools.partial(_post_body, last_layer),
        grid=(n // tm,),
        in_specs=[row(d), row(att.shape[1]), row(c.shape[1]), row(p.shape[1])]
                 + [_resident(prm[k].shape) for k in names],
        out_specs=row(d),
        out_shape=jax.ShapeDtypeStruct((n, d), F32),
        compiler_params=_params(1),
        name="post",
    )(h, att, c, p, *[prm[k] for k in names])


def _rope_tables(pos):
    half = QK_ROPE // 2
    inv = ROPE_BASE ** (-jnp.arange(half, dtype=F32) / half)
    ang = pos.astype(F32)[:, None] * inv[None, :]
    cos, sin = jnp.cos(ang), jnp.sin(ang)
    n = pos.shape[0]
    cos_t = jnp.concatenate([cos, cos, jnp.ones((n, QK_NOPE), F32), jnp.zeros((n, HEAD_PAD - QK_ROPE - QK_NOPE), F32)],
                            axis=1)
    sin_t = jnp.concatenate([-sin, sin, jnp.zeros((n, HEAD_PAD - QK_ROPE), F32)], axis=1)
    return cos_t, sin_t


def _swap_halves(w):
    half = QK_ROPE // 2
    return jnp.concatenate([w[..., half:], w[..., :half]], axis=-1)


def _prep_layer(l, g_ff1, w1_gate, w1_up, w1_down, g_mix, w_in, g_q, w_q_up, g_kv, w_uk, w_uv, conv_w, conv_b,
                g_cn, b_cn, w_o, g_ff2, w2_gate, w2_up, w2_down, g_ple, w_pg, b_pg, w_pp, g_final):
    vec = lambda a: a.reshape(1, -1)
    d_model = w_in.shape[1]
    prm = dict(g_ff1=vec(g_ff1[l]), w1_gate=w1_gate[l].astype(BF16), w1_up=w1_up[l].astype(BF16),
               w1_down=w1_down[l].astype(BF16), g_mix=vec(g_mix[l]), g_q=vec(g_q[l]), g_kv=vec(g_kv[l]),
               conv_w=conv_w[l], conv_b=vec(conv_b[l]), g_cn=vec(g_cn[l]), b_cn=vec(b_cn[l]),
               g_ff2=vec(g_ff2[l]), w2_gate=w2_gate[l].astype(BF16), w2_up=w2_up[l].astype(BF16),
               w2_down=w2_down[l].astype(BF16), g_ple=vec(g_ple[l]), w_pg=w_pg[l].astype(BF16),
               b_pg=vec(b_pg[l]), w_pp=w_pp[l].astype(BF16), g_final=vec(g_final))
    wi = w_in[l]
    o1, o2, o3 = Q_RANK, Q_RANK + KV_RANK, Q_RANK + KV_RANK + QK_ROPE
    w_kpe = wi[:, o2:o3]
    lane_pad = jnp.zeros((d_model, LANES - QK_ROPE), F32)
    prm["w_in"] = jnp.concatenate(
        [wi[:, :o2], wi[:, o3:], w_kpe, lane_pad, _swap_halves(w_kpe), lane_pad], axis=1).astype(BF16)
    wq = w_q_up[l]
    wq_nope, wq_pe = wq[..., :QK_NOPE], wq[..., QK_NOPE:]
    zq = lambda w: jnp.zeros(wq.shape[:2] + (w,), F32)
    prm["w_q"] = jnp.concatenate([wq_pe, wq_nope, zq(HEAD_PAD - QK_ROPE - QK_NOPE)], axis=-1
                                 ).reshape(Q_RANK, N_HEADS * HEAD_PAD).astype(BF16)
    prm["w_qs"] = jnp.concatenate([_swap_halves(wq_pe), zq(HEAD_PAD - QK_ROPE)], axis=-1
                                  ).reshape(Q_RANK, N_HEADS * HEAD_PAD).astype(BF16)
    wuk = w_uk[l]
    zk = lambda w: jnp.zeros(wuk.shape[:2] + (w,), F32)
    wuk_pad = jnp.concatenate([zk(QK_ROPE), wuk, zk(HEAD_PAD - QK_ROPE - QK_NOPE)], axis=-1)
    prm["w_uk"] = wuk_pad.reshape(KV_RANK, N_HEADS * HEAD_PAD).astype(BF16)
    prm["w_ukt"] = jnp.transpose(wuk_pad, (1, 2, 0)).astype(BF16)
    wuv = w_uv[l]
    prm["w_uv"] = wuv.reshape(KV_RANK, N_HEADS * V_HEAD).astype(BF16)
    eye = jnp.eye(N_HEADS, dtype=F32)
    prm["w_uv_pad"] = jnp.einsum("rhv,hg->hrgv", wuv, eye).reshape(N_HEADS, KV_RANK, N_HEADS * V_HEAD).astype(BF16)
    d_att = N_HEADS * V_HEAD
    prm["w_o_att"] = w_o[l][:d_att].astype(BF16)
    prm["w_o_conv"] = w_o[l][d_att:].astype(BF16)
    return prm


TM_FFN = 512
TM_PROJ = 512
TM_POST = 512
TQ_FLASH = 512
FLASH_UNROLL = 4
SAMPLE_CONV_BATCH = 16
SAMPLE_PAGES_PER_STEP = 128
SAMPLE_PAGES_PER_SOFTMAX = 32


def kernel(x_prompt, x_sample, cache_ckv, cache_kpe, state_conv, page_table, p_prompt, p_sample, g_ff1, w1_gate,
           w1_up, w1_down, g_mix, w_in, g_q, w_q_up, g_kv, w_uk, w_uv, conv_w, conv_b, g_cn, b_cn, w_o, g_ff2,
           w2_gate, w2_up, w2_down, g_ple, w_pg, b_pg, w_pp, g_final):
    b_pr, t_pr, d_model = x_prompt.shape
    b_dec, t_dec, _ = x_sample.shape
    depth = w_in.shape[0]
    past_len = page_table.shape[1] * cache_ckv.shape[2]
    n_pr, n_dec = b_pr * t_pr, b_dec * t_dec
    cos_p, sin_p = _rope_tables(jnp.arange(t_pr))
    cos_s, sin_s = _rope_tables(past_len + jnp.arange(n_dec) % t_dec)

    h_p = x_prompt.reshape(n_pr, d_model)
    h_s = x_sample.reshape(n_dec, d_model)
    outs = [[] for _ in range(6)]
    for l in range(depth):
        prm = _prep_layer(l, g_ff1, w1_gate, w1_up, w1_down, g_mix, w_in, g_q, w_q_up, g_kv, w_uk, w_uv, conv_w,
                          conv_b, g_cn, b_cn, w_o, g_ff2, w2_gate, w2_up, w2_down, g_ple, w_pg, b_pg, w_pp,
                          g_final)
        h1 = _ffn(h_p, prm["g_ff1"], prm["w1_gate"], prm["w1_up"], prm["w1_down"], TM_FFN)
        conv_zero = jnp.zeros((b_pr, CONV_WIDTH - 1, CONV_CH), F32)
        ckv, kpe, c, conv_state, q, k, v = _inproj(h1, prm, cos_p, sin_p, TM_PROJ, prompt_bt=(b_pr, t_pr),
                                                   hist=conv_zero)
        att = _flash(q, k, v, TQ_FLASH).reshape(n_pr, -1)
        h_p = _post(h1, att, c, p_prompt[l].reshape(n_pr, -1), prm, TM_POST,
                    l == depth - 1)
        outs[0].append(ckv.reshape(b_pr, t_pr, KV_RANK))
        outs[1].append(kpe.reshape(b_pr, t_pr, QK_ROPE))
        outs[2].append(conv_state)

        h1 = _ffn(h_s, prm["g_ff1"], prm["w1_gate"], prm["w1_up"], prm["w1_down"], n_dec)
        ckv, kpe, u, q_abs, q_pe = _inproj(h1, prm, cos_s, sin_s, n_dec)
        rows = N_HEADS * t_dec
        per_req = lambda a: jnp.transpose(a.reshape(N_HEADS, b_dec, t_dec, a.shape[-1]), (1, 0, 2, 3)
                                          ).reshape(b_dec, rows, a.shape[-1])
        ckv_new = ckv.reshape(b_dec, t_dec, KV_RANK)
        kpe_new = kpe.reshape(b_dec, t_dec, QK_ROPE)
        o_lat = _sattn(page_table, per_req(q_abs), per_req(q_pe[..., :QK_ROPE]), ckv_new, kpe_new,
                       cache_ckv[l], jnp.swapaxes(cache_kpe[l], 1, 2), SAMPLE_PAGES_PER_STEP)
        o_lat = jnp.transpose(o_lat.reshape(b_dec, N_HEADS, t_dec, KV_RANK), (1, 0, 2, 3)
                              ).reshape(N_HEADS, n_dec, KV_RANK)
        att = _uv(o_lat, prm["w_uv_pad"])
        c, conv_state = _conv(u.reshape(b_dec, t_dec, CONV_CH), state_conv[l], prm["conv_w"], prm["conv_b"],
                              prm["g_cn"], prm["b_cn"], SAMPLE_CONV_BATCH, t_dec)
        h_s = _post(h1, att, c.reshape(n_dec, CONV_CH), p_sample[l].reshape(n_dec, -1), prm, TM_POST,
                    l == depth - 1)
        outs[3].append(ckv_new)
        outs[4].append(kpe_new)
        outs[5].append(conv_state)

    y_prompt = h_p.reshape(b_pr, t_pr, d_model)
    y_sample = h_s.reshape(b_dec, t_dec, d_model)
    return (y_prompt, y_sample, jnp.stack(outs[0]), jnp.stack(outs[1]), jnp.stack(outs[2]),
            jnp.stack(outs[3]), jnp.stack(outs[4]), jnp.stack(outs[5]))
```

```python
import functools

import jax
import jax.numpy as jnp
from jax import lax
from jax.experimental import pallas as pl
from jax.experimental.pallas import tpu as pltpu

F32 = jnp.float32
BF16 = jnp.bfloat16

N_HEADS = 8
QK_NOPE = 64
QK_ROPE = 32
V_HEAD = 64
Q_RANK = 384
KV_RANK = 256
CONV_CH = 512
CONV_WIDTH = 31
ROPE_BASE = 10000.0
EPS = 1e-6
SCALE = (QK_NOPE + QK_ROPE) ** -0.5
LOG2_E = 1.4426950408889634
Q_SCALE = SCALE * LOG2_E
NEG_INF = -1e30

LANES = 128
SUBLANES = 8
HEAD_PAD = LANES
CONV_HALO = 32
VMEM_LIMIT = 56 * 1024 * 1024


def _rms(x, g):
    return x * lax.rsqrt(jnp.mean(x * x, axis=-1, keepdims=True) + EPS) * g


def _dot(a, b):
    return jnp.dot(a, b, preferred_element_type=F32)


def _dot_t(a, b):
    return lax.dot_general(a, b, (((1,), (1,)), ((), ())), preferred_element_type=F32)


def _resident(shape):
    nd = len(shape)
    return pl.BlockSpec(shape, lambda *_: (0,) * nd, pipeline_mode=pl.Buffered(1))


def _params(n_axes, flags=None):
    return pltpu.CompilerParams(dimension_semantics=("arbitrary",) * n_axes,
                                vmem_limit_bytes=VMEM_LIMIT, flags=flags)


def _swiglu(n, wg_ref, wu_ref, wd_ref):
    gate = _dot(n, wg_ref[...])
    up = _dot(n, wu_ref[...])
    a = (gate * jax.nn.sigmoid(gate) * up).astype(BF16)
    return _dot(a, wd_ref[...])


def _ffn_body(x_ref, g_ref, wg_ref, wu_ref, wd_ref, o_ref):
    x = x_ref[...]
    n = _rms(x, g_ref[...]).astype(BF16)
    o_ref[...] = x + 0.5 * _swiglu(n, wg_ref, wu_ref, wd_ref)


def _ffn(x, g, wg, wu, wd, tm):
    n, d = x.shape
    row = pl.BlockSpec((tm, d), lambda i: (i, 0))
    return pl.pallas_call(
        _ffn_body,
        grid=(n // tm,),
        in_specs=[row, _resident(g.shape), _resident(wg.shape), _resident(wu.shape), _resident(wd.shape)],
        out_specs=row,
        out_shape=jax.ShapeDtypeStruct((n, d), F32),
        compiler_params=_params(1),
        name="ffn1",
    )(x, g, wg, wu, wd)


_O_CQ = 0
_O_CKV = _O_CQ + Q_RANK
_O_UA = _O_CKV + KV_RANK
_O_UB = _O_UA + CONV_CH
_O_KPE = _O_UB + CONV_CH
_O_KPES = _O_KPE + LANES
_W_IN_COLS = _O_KPES + LANES


def _inproj_common(h_ref, gmix_ref, win_ref, gq_ref, wq_ref, wqs_ref, gkv_ref, cos_ref, sin_ref,
                   ckv_ref, kpe_ref):
    n = _rms(h_ref[...], gmix_ref[...]).astype(BF16)
    z = _dot(n, win_ref[...])
    cos = cos_ref[...]
    sin = sin_ref[...]
    cqn = _rms(z[:, _O_CQ:_O_CKV], gq_ref[...]).astype(BF16)
    q = _dot(cqn, wq_ref[...])
    qs = _dot(cqn, wqs_ref[...])
    ckv = _rms(z[:, _O_CKV:_O_UA], gkv_ref[...])
    kpe = z[:, _O_KPE:_O_KPES] * cos + z[:, _O_KPES:_W_IN_COLS] * sin
    ckv_ref[...] = ckv
    kpe_ref[...] = kpe[:, :QK_ROPE]
    u = z[:, _O_UA:_O_UB] * jax.nn.sigmoid(z[:, _O_UB:_O_KPE])
    q_heads = []
    for h in range(N_HEADS):
        hs = slice(h * HEAD_PAD, (h + 1) * HEAD_PAD)
        q_heads.append((q[:, hs] * cos + qs[:, hs] * sin) * Q_SCALE)
    return q_heads, ckv, kpe, u


def _inproj_prompt_body(tpb, h_ref, gmix_ref, win_ref, gq_ref, wq_ref, wqs_ref, gkv_ref, cos_ref, sin_ref,
                        wuk_ref, wuv_ref, hist_ref, cw_ref, cb_ref, gcn_ref, bcn_ref,
                        ckv_ref, kpe_ref, c_ref, state_ref, q_ref, k_ref, v_ref, ext_ref, rot_ref):
    _conv_init(pl.program_id(0) == 0, h_ref.shape[0], ext_ref)
    q_heads, ckv, kpe, u = _inproj_common(h_ref, gmix_ref, win_ref, gq_ref, wq_ref, wqs_ref, gkv_ref,
                                          cos_ref, sin_ref, ckv_ref, kpe_ref)
    _conv_tile(u.shape[0], pl.program_id(0) % tpb == 0, u[None], hist_ref, cw_ref, cb_ref, gcn_ref, bcn_ref,
               c_ref, state_ref, ext_ref, rot_ref)
    ckvb = ckv.astype(BF16)
    kn = _dot(ckvb, wuk_ref[...])
    vv = _dot(ckvb, wuv_ref[...])
    for h in range(N_HEADS):
        hs = slice(h * HEAD_PAD, (h + 1) * HEAD_PAD)
        q_ref[0, h] = q_heads[h].astype(BF16)
        k_ref[0, h] = (kn[:, hs] + kpe).astype(BF16)
    lane = lax.broadcasted_iota(jnp.int32, (vv.shape[0], LANES), 1)
    for hp in range(N_HEADS // 2):
        pair = vv[:, hp * LANES:(hp + 1) * LANES]
        v_ref[0, 2 * hp] = jnp.where(lane < V_HEAD, pair, (lane == V_HEAD).astype(F32)).astype(BF16)
        v_ref[0, 2 * hp + 1] = jnp.where(lane >= V_HEAD, pair, (lane == 0).astype(F32)).astype(BF16)


def _inproj_sample_body(h_ref, gmix_ref, win_ref, gq_ref, wq_ref, wqs_ref, gkv_ref, cos_ref, sin_ref,
                        wukt_ref,
                        ckv_ref, kpe_ref, u_ref, qabs_ref, qpe_ref):
    q_heads, _, _, u = _inproj_common(h_ref, gmix_ref, win_ref, gq_ref, wq_ref, wqs_ref, gkv_ref,
                                      cos_ref, sin_ref, ckv_ref, kpe_ref)
    u_ref[...] = u
    for h in range(N_HEADS):
        qh = q_heads[h].astype(BF16)
        qpe_ref[h] = qh
        qabs_ref[h] = _dot(qh, wukt_ref[h]).astype(BF16)


def _inproj(h, prm, cos, sin, tm, prompt_bt=None, hist=None):
    n, d = h.shape
    tab_tiles = cos.shape[0] // tm
    row = lambda w: pl.BlockSpec((tm, w), lambda i: (i, 0))
    tab = pl.BlockSpec((tm, LANES), lambda i: (i % tab_tiles, 0))
    common_in = [row(d), _resident(prm["g_mix"].shape), _resident(prm["w_in"].shape),
                 _resident(prm["g_q"].shape), _resident(prm["w_q"].shape), _resident(prm["w_qs"].shape),
                 _resident(prm["g_kv"].shape), tab, tab]
    common_args = (h, prm["g_mix"], prm["w_in"], prm["g_q"], prm["w_q"], prm["w_qs"], prm["g_kv"], cos, sin)
    common_out = [row(KV_RANK), row(QK_ROPE)]
    common_shape = [jax.ShapeDtypeStruct((n, KV_RANK), F32), jax.ShapeDtypeStruct((n, QK_ROPE), F32)]
    if prompt_bt is not None:
        b, t = prompt_bt
        tpb = t // tm
        head = lambda nh: pl.BlockSpec((1, nh, tm, LANES), lambda i: (i // tpb, 0, i % tpb, 0))
        conv_names = ["conv_w", "conv_b", "g_cn", "b_cn"]
        per_seq = lambda rows: pl.BlockSpec((1, rows, CONV_CH), lambda i: (i // tpb, 0, 0))
        return pl.pallas_call(
            functools.partial(_inproj_prompt_body, tpb),
            grid=(n // tm,),
            in_specs=common_in + [_resident(prm["w_uk"].shape), _resident(prm["w_uv"].shape), per_seq(CONV_HALO)]
                     + [_resident(prm[k].shape) for k in conv_names],
            out_specs=common_out + [row(CONV_CH), per_seq(CONV_WIDTH - 1)] + [head(N_HEADS)] * 3,
            out_shape=common_shape + [jax.ShapeDtypeStruct((n, CONV_CH), BF16),
                                      jax.ShapeDtypeStruct((b, CONV_WIDTH - 1, CONV_CH), F32)]
                      + [jax.ShapeDtypeStruct((b, N_HEADS, t, LANES), BF16)] * 3,
            scratch_shapes=_conv_scratch(1, tm, CONV_CH),
            compiler_params=_params(1),
            name="inproj_prompt",
        )(*common_args, prm["w_uk"], prm["w_uv"], _pad_hist(hist), *[prm[k] for k in conv_names])
    hrow = lambda w: pl.BlockSpec((N_HEADS, tm, w), lambda i: (0, i, 0))
    return pl.pallas_call(
        _inproj_sample_body,
        grid=(n // tm,),
        in_specs=common_in + [_resident(prm["w_ukt"].shape)],
        out_specs=common_out + [row(CONV_CH), hrow(KV_RANK), hrow(LANES)],
        out_shape=common_shape + [jax.ShapeDtypeStruct((n, CONV_CH), F32),
                                  jax.ShapeDtypeStruct((N_HEADS, n, KV_RANK), BF16),
                                  jax.ShapeDtypeStruct((N_HEADS, n, LANES), BF16)],
        compiler_params=_params(1),
        name="inproj_sample",
    )(*common_args, prm["w_ukt"])


def _conv_init(first_step, tt, ext_ref):
    @pl.when(first_step)
    def _():
        ext_ref[:, tt:tt + CONV_HALO, :] = jnp.zeros((ext_ref.shape[0], CONV_HALO, ext_ref.shape[2]), F32)


def _conv_tile(tt, first, v, hist_ref, w_ref, b_ref, g_ref, beta_ref, c_ref, state_ref, ext_ref, rot_ref):
    ext_ref[:, 0:CONV_HALO, :] = jnp.where(first, hist_ref[...], ext_ref[:, tt:tt + CONV_HALO, :])
    ext_ref[:, CONV_HALO:CONV_HALO + tt, :] = v
    off = CONV_HALO - (CONV_WIDTH - 1)
    span = rot_ref.shape[2]
    for r in range(1, SUBLANES):
        rot_ref[r - 1] = ext_ref[:, r:r + span, :]

    def tap(k):
        r = (off + k) % SUBLANES
        base = off + k - r
        win = ext_ref[:, base:base + tt, :] if r == 0 else rot_ref[r - 1, :, base:base + tt, :]
        return win * w_ref[k:k + 1, :]

    acc = tap(0)
    for k in range(1, CONV_WIDTH):
        acc = acc + tap(k)
    acc = acc + b_ref[...]
    mu = jnp.mean(acc, axis=-1, keepdims=True)
    cen = acc - mu
    var = jnp.mean(cen * cen, axis=-1, keepdims=True)
    y = cen * lax.rsqrt(var + EPS) * g_ref[...] + beta_ref[...]
    c_ref[...] = (y * jax.nn.sigmoid(y)).astype(BF16).reshape(c_ref.shape)
    state_ref[...] = ext_ref[:, tt + off:tt + CONV_HALO, :]


def _conv_scratch(nb, tt, ch):
    return [pltpu.VMEM((nb, tt + CONV_HALO, ch), F32),
            pltpu.VMEM((SUBLANES - 1, nb, tt + CONV_HALO - SUBLANES, ch), F32)]


def _pad_hist(hist):
    return jnp.pad(hist, ((0, 0), (CONV_HALO - hist.shape[1], 0), (0, 0)))


def _conv_body(tt, v_ref, hist_ref, w_ref, b_ref, g_ref, beta_ref, c_ref, state_ref, ext_ref, rot_ref):
    ti = pl.program_id(1)
    _conv_init(jnp.logical_and(pl.program_id(0) == 0, ti == 0), tt, ext_ref)
    _conv_tile(tt, ti == 0, v_ref[...], hist_ref, w_ref, b_ref, g_ref, beta_ref, c_ref, state_ref, ext_ref, rot_ref)


def _conv(v, hist, w, b, g, beta, nb, tt):
    bsz, t, ch = v.shape
    hist = _pad_hist(hist)
    return pl.pallas_call(
        functools.partial(_conv_body, tt),
        grid=(bsz // nb, t // tt),
        in_specs=[pl.BlockSpec((nb, tt, ch), lambda i, j: (i, j, 0)),
                  pl.BlockSpec((nb, CONV_HALO, ch), lambda i, j: (i, 0, 0)),
                  _resident(w.shape), _resident(b.shape), _resident(g.shape), _resident(beta.shape)],
        out_specs=[pl.BlockSpec((nb, tt, ch), lambda i, j: (i, j, 0)),
                   pl.BlockSpec((nb, CONV_WIDTH - 1, ch), lambda i, j: (i, 0, 0))],
        out_shape=[jax.ShapeDtypeStruct((bsz, t, ch), BF16),
                   jax.ShapeDtypeStruct((bsz, CONV_WIDTH - 1, ch), F32)],
        scratch_shapes=_conv_scratch(nb, tt, ch),
        compiler_params=_params(2),
        name="conv",
    )(v, hist, w, b, g, beta)


def _flash_body(tq, unroll, q_ref, k_ref, v_ref, o_ref):
    qi = pl.program_id(2)
    qs = (q_ref[0, 0], q_ref[0, 1])

    def absorb(j, state, masked):
        start = pl.multiple_of(j * tq, tq)
        out = []
        for hh in range(2):
            m_prev, acc = state[hh]
            s = _dot_t(qs[hh], k_ref[0, hh, pl.ds(start, tq), :])
            if masked:
                row = lax.broadcasted_iota(jnp.int32, (tq, tq), 0)
                col = lax.broadcasted_iota(jnp.int32, (tq, tq), 1)
                s = jnp.where(row >= col, s, NEG_INF)
            m_new = jnp.maximum(m_prev, jnp.max(s, axis=-1, keepdims=True))
            p = jnp.exp2(s - m_new).astype(BF16)
            pv = _dot(p, v_ref[0, hh, pl.ds(start, tq), :])
            out.append((m_new, jnp.exp2(m_prev - m_new) * acc + pv))
        return tuple(out)

    def trip(g, state):
        for u in range(unroll):
            state = absorb(g * unroll + u, state, False)
        return state

    init_h = (jnp.full((tq, 1), NEG_INF, F32), jnp.zeros((tq, LANES), F32))
    state = lax.fori_loop(0, qi // unroll, trip, (init_h, init_h))
    u = unroll // 2
    while u >= 1:
        base = (qi // (2 * u)) * (2 * u)

        def leftover(st, base=base, u=u):
            for i in range(u):
                st = absorb(base + i, st, False)
            return st

        state = lax.cond((qi // u) % 2 == 1, leftover, lambda st: st, state)
        u //= 2
    (_, acc0), (_, acc1) = absorb(qi, state, True)
    lane = lax.broadcasted_iota(jnp.int32, (tq, LANES), 1)
    o_ref[0] = jnp.where(lane < V_HEAD, acc0 / acc0[:, V_HEAD:V_HEAD + 1], acc1 / acc1[:, 0:1]).astype(BF16)


def _flash(q, k, v, tq):
    b, nh, t, _ = q.shape
    return pl.pallas_call(
        functools.partial(_flash_body, tq, FLASH_UNROLL),
        grid=(b, nh // 2, t // tq),
        in_specs=[pl.BlockSpec((1, 2, tq, LANES), lambda bi, hp, qi: (bi, hp, qi, 0)),
                  pl.BlockSpec((1, 2, t, LANES), lambda bi, hp, qi: (bi, hp, 0, 0)),
                  pl.BlockSpec((1, 2, t, LANES), lambda bi, hp, qi: (bi, hp, 0, 0))],
        out_specs=pl.BlockSpec((1, tq, LANES), lambda bi, hp, qi: (bi, qi, hp)),
        out_shape=jax.ShapeDtypeStruct((b, t, nh * V_HEAD), BF16),
        compiler_params=_params(3),
        name="flash_prompt",
    )(q, k, v)


def _sattn_body(ch, sub, t_new, pt_ref, q_ref, qpe_ref, cnew_ref, knew_ref, ckv_hbm, kpe_hbm, o_ref,
                cbuf, kbuf, sem, m_ref, l_ref, acc_ref):
    b = pl.program_id(0)
    c = pl.program_id(1)
    nc = pl.num_programs(1)
    total = pl.num_programs(0) * nc
    step = b * nc + c
    slot = step % 2
    page = cbuf.shape[1] // ch

    def page_copies(bb, cc, sl):
        for j in range(ch):
            pg = pt_ref[bb, cc * ch + j]
            dst = pl.ds(j * page, page)
            yield pltpu.make_async_copy(ckv_hbm.at[pg], cbuf.at[sl, dst, :], sem.at[0, sl])
            yield pltpu.make_async_copy(kpe_hbm.at[pg], kbuf.at[sl, j], sem.at[1, sl])

    @pl.when(step == 0)
    def _():
        for cp in page_copies(0, 0, 0):
            cp.start()

    @pl.when(step + 1 < total)
    def _():
        nxt = step + 1
        for cp in page_copies(nxt // nc, nxt % nc, 1 - slot):
            cp.start()

    @pl.when(c == 0)
    def _():
        m_ref[...] = jnp.full(m_ref.shape, NEG_INF, F32)
        l_ref[...] = jnp.zeros(l_ref.shape, F32)
        acc_ref[...] = jnp.zeros(acc_ref.shape, F32)

    for cp in page_copies(b, c, slot):
        cp.wait()

    q = q_ref[0]
    qpe = qpe_ref[0]
    m_run, l_run, acc = m_ref[...], l_ref[...], acc_ref[...]
    for g in range(ch // sub):
        ck = cbuf[slot, g * sub * page:(g + 1) * sub * page, :].astype(BF16)
        s_pe = jnp.concatenate([_dot(qpe, kbuf[slot, g * sub + j].astype(BF16)) for j in range(sub)], axis=1)
        s = _dot_t(q, ck) + s_pe
        m_new = jnp.maximum(m_run, jnp.max(s, axis=-1, keepdims=True))
        alpha = jnp.exp2(m_run - m_new)
        p = jnp.exp2(s - m_new)
        l_run = alpha * l_run + jnp.sum(p, axis=-1, keepdims=True)
        acc = alpha * acc + _dot(p.astype(BF16), ck)
        m_run = m_new
    l_ref[...] = l_run
    acc_ref[...] = acc
    m_ref[...] = m_run

    @pl.when(c == nc - 1)
    def _():
        qf = q.astype(F32)
        qpf = qpe.astype(F32)
        cn = cnew_ref[0].astype(BF16).astype(F32)
        kn = knew_ref[0].astype(BF16).astype(F32)
        t_row = lax.broadcasted_iota(jnp.int32, (q.shape[0], 1), 0) % t_new
        s_new = []
        for j in range(t_new):
            sj = (jnp.sum(qf * cn[j:j + 1, :], axis=-1, keepdims=True)
                  + jnp.sum(qpf * kn[j:j + 1, :], axis=-1, keepdims=True))
            s_new.append(jnp.where(t_row >= j, sj, NEG_INF))
        m_old = m_ref[...]
        m_fin = m_old
        for sj in s_new:
            m_fin = jnp.maximum(m_fin, sj)
        a_fin = jnp.exp2(m_old - m_fin)
        l_fin = a_fin * l_ref[...]
        acc = a_fin * acc_ref[...]
        for j, sj in enumerate(s_new):
            pj = jnp.exp2(sj - m_fin)
            l_fin = l_fin + pj
            acc = acc + pj.astype(BF16).astype(F32) * cn[j:j + 1, :]
        o_ref[0] = acc / l_fin


def _sattn(page_table, q, qpe, ckv_new, kpe_new, cache_ckv, cache_kpe, ch):
    bsz, rows, _ = q.shape
    n_pages = page_table.shape[1]
    page = cache_ckv.shape[1]
    t_new = ckv_new.shape[1]
    grid_spec = pltpu.PrefetchScalarGridSpec(
        num_scalar_prefetch=1,
        grid=(bsz, n_pages // ch),
        in_specs=[pl.BlockSpec((1, rows, KV_RANK), lambda b, c, pt: (b, 0, 0)),
                  pl.BlockSpec((1, rows, QK_ROPE), lambda b, c, pt: (b, 0, 0)),
                  pl.BlockSpec((1, t_new, KV_RANK), lambda b, c, pt: (b, 0, 0)),
                  pl.BlockSpec((1, t_new, QK_ROPE), lambda b, c, pt: (b, 0, 0)),
                  pl.BlockSpec(memory_space=pl.ANY),
                  pl.BlockSpec(memory_space=pl.ANY)],
        out_specs=pl.BlockSpec((1, rows, KV_RANK), lambda b, c, pt: (b, 0, 0)),
        scratch_shapes=[pltpu.VMEM((2, ch * page, KV_RANK), F32),
                        pltpu.VMEM((2, ch, QK_ROPE, page), F32),
                        pltpu.SemaphoreType.DMA((2, 2)),
                        pltpu.VMEM((rows, 1), F32),
                        pltpu.VMEM((rows, 1), F32),
                        pltpu.VMEM((rows, KV_RANK), F32)],
    )
    return pl.pallas_call(
        functools.partial(_sattn_body, ch, SAMPLE_PAGES_PER_SOFTMAX, t_new),
        grid_spec=grid_spec,
        out_shape=jax.ShapeDtypeStruct((bsz, rows, KV_RANK), F32),
        compiler_params=_params(2),
        name="attn_sample",
    )(page_table, q, qpe, ckv_new, kpe_new, cache_ckv, cache_kpe)


def _uv_body(o_ref, w_ref, att_ref):
    acc = _dot(o_ref[0].astype(BF16), w_ref[0])
    for h in range(1, N_HEADS):
        acc = acc + _dot(o_ref[h].astype(BF16), w_ref[h])
    att_ref[...] = acc.astype(BF16)


def _uv(o_lat, w_uv_pad):
    _, n, _ = o_lat.shape
    return pl.pallas_call(
        _uv_body,
        grid=(1,),
        in_specs=[_resident(o_lat.shape), _resident(w_uv_pad.shape)],
        out_specs=pl.BlockSpec((n, N_HEADS * V_HEAD), lambda i: (0, 0)),
        out_shape=jax.ShapeDtypeStruct((n, N_HEADS * V_HEAD), BF16),
        compiler_params=_params(1),
        name="uv_sample",
    )(o_lat, w_uv_pad)


def _post_body(last_layer, h_ref, att_ref, c_ref, p_ref, woa_ref, woc_ref, g2_ref, wg_ref, wu_ref, wd_ref,
               gple_ref, wpg_ref, bpg_ref, wpp_ref, gfin_ref, y_ref):
    h = h_ref[...] + _dot(att_ref[...], woa_ref[...]) + _dot(c_ref[...], woc_ref[...])
    h = h + 0.5 * _swiglu(_rms(h, g2_ref[...]).astype(BF16), wg_ref, wu_ref, wd_ref)
    gate = jax.nn.sigmoid(_dot(_rms(h, gple_ref[...]).astype(BF16), wpg_ref[...]) + bpg_ref[...])
    h = h + gate * _dot(p_ref[...].astype(BF16), wpp_ref[...])
    y_ref[...] = _rms(h, gfin_ref[...]) if last_layer else h


def _post(h, att, c, p, prm, tm, last_layer):
    n, d = h.shape
    row = lambda w: pl.BlockSpec((tm, w), lambda i: (i, 0))
    names = ["w_o_att", "w_o_conv", "g_ff2", "w2_gate", "w2_up", "w2_down", "g_ple", "w_pg", "b_pg", "w_pp",
             "g_final"]
    return pl.pallas_call(
        functools.partial(_post_body, last_layer),
        grid=(n // tm,),
        in_specs=[row(d), row(att.shape[1]), row(c.shape[1]), row(p.shape[1])]
                 + [_resident(prm[k].shape) for k in names],
        out_specs=row(d),
        out_shape=jax.ShapeDtypeStruct((n, d), F32),
        compiler_params=_params(1),
        name="post",
    )(h, att, c, p, *[prm[k] for k in names])


def _rope_tables(pos):
    half = QK_ROPE // 2
    inv = ROPE_BASE ** (-jnp.arange(half, dtype=F32) / half)
    ang = pos.astype(F32)[:, None] * inv[None, :]
    cos, sin = jnp.cos(ang), jnp.sin(ang)
    n = pos.shape[0]
    cos_t = jnp.concatenate([cos, cos, jnp.ones((n, QK_NOPE), F32), jnp.zeros((n, HEAD_PAD - QK_ROPE - QK_NOPE), F32)],
                            axis=1)
    sin_t = jnp.concatenate([-sin, sin, jnp.zeros((n, HEAD_PAD - QK_ROPE), F32)], axis=1)
    return cos_t, sin_t


def _swap_halves(w):
    half = QK_ROPE // 2
    return jnp.concatenate([w[..., half:], w[..., :half]], axis=-1)


def _prep_layer(l, g_ff1, w1_gate, w1_up, w1_down, g_mix, w_in, g_q, w_q_up, g_kv, w_uk, w_uv, conv_w, conv_b,
                g_cn, b_cn, w_o, g_ff2, w2_gate, w2_up, w2_down, g_ple, w_pg, b_pg, w_pp, g_final):
    vec = lambda a: a.reshape(1, -1)
    d_model = w_in.shape[1]
    prm = dict(g_ff1=vec(g_ff1[l]), w1_gate=w1_gate[l].astype(BF16), w1_up=w1_up[l].astype(BF16),
               w1_down=w1_down[l].astype(BF16), g_mix=vec(g_mix[l]), g_q=vec(g_q[l]), g_kv=vec(g_kv[l]),
               conv_w=conv_w[l], conv_b=vec(conv_b[l]), g_cn=vec(g_cn[l]), b_cn=vec(b_cn[l]),
               g_ff2=vec(g_ff2[l]), w2_gate=w2_gate[l].astype(BF16), w2_up=w2_up[l].astype(BF16),
               w2_down=w2_down[l].astype(BF16), g_ple=vec(g_ple[l]), w_pg=w_pg[l].astype(BF16),
               b_pg=vec(b_pg[l]), w_pp=w_pp[l].astype(BF16), g_final=vec(g_final))
    wi = w_in[l]
    o1, o2, o3 = Q_RANK, Q_RANK + KV_RANK, Q_RANK + KV_RANK + QK_ROPE
    w_kpe = wi[:, o2:o3]
    lane_pad = jnp.zeros((d_model, LANES - QK_ROPE), F32)
    prm["w_in"] = jnp.concatenate(
        [wi[:, :o2], wi[:, o3:], w_kpe, lane_pad, _swap_halves(w_kpe), lane_pad], axis=1).astype(BF16)
    wq = w_q_up[l]
    wq_nope, wq_pe = wq[..., :QK_NOPE], wq[..., QK_NOPE:]
    zq = lambda w: jnp.zeros(wq.shape[:2] + (w,), F32)
    prm["w_q"] = jnp.concatenate([wq_pe, wq_nope, zq(HEAD_PAD - QK_ROPE - QK_NOPE)], axis=-1
                                 ).reshape(Q_RANK, N_HEADS * HEAD_PAD).astype(BF16)
    prm["w_qs"] = jnp.concatenate([_swap_halves(wq_pe), zq(HEAD_PAD - QK_ROPE)], axis=-1
                                  ).reshape(Q_RANK, N_HEADS * HEAD_PAD).astype(BF16)
    wuk = w_uk[l]
    zk = lambda w: jnp.zeros(wuk.shape[:2] + (w,), F32)
    wuk_pad = jnp.concatenate([zk(QK_ROPE), wuk, zk(HEAD_PAD - QK_ROPE - QK_NOPE)], axis=-1)
    prm["w_uk"] = wuk_pad.reshape(KV_RANK, N_HEADS * HEAD_PAD).astype(BF16)
    prm["w_ukt"] = jnp.transpose(wuk_pad, (1, 2, 0)).astype(BF16)
    wuv = w_uv[l]
    prm["w_uv"] = wuv.reshape(KV_RANK, N_HEADS * V_HEAD).astype(BF16)
    eye = jnp.eye(N_HEADS, dtype=F32)
    prm["w_uv_pad"] = jnp.einsum("rhv,hg->hrgv", wuv, eye).reshape(N_HEADS, KV_RANK, N_HEADS * V_HEAD).astype(BF16)
    d_att = N_HEADS * V_HEAD
    prm["w_o_att"] = w_o[l][:d_att].astype(BF16)
    prm["w_o_conv"] = w_o[l][d_att:].astype(BF16)
    return prm


TM_FFN = 512
TM_PROJ = 512
TM_POST = 512
TQ_FLASH = 512
FLASH_UNROLL = 8
SAMPLE_CONV_BATCH = 16
SAMPLE_PAGES_PER_STEP = 128
SAMPLE_PAGES_PER_SOFTMAX = 32


def kernel(x_prompt, x_sample, cache_ckv, cache_kpe, state_conv, page_table, p_prompt, p_sample, g_ff1, w1_gate,
           w1_up, w1_down, g_mix, w_in, g_q, w_q_up, g_kv, w_uk, w_uv, conv_w, conv_b, g_cn, b_cn, w_o, g_ff2,
           w2_gate, w2_up, w2_down, g_ple, w_pg, b_pg, w_pp, g_final):
    b_pr, t_pr, d_model = x_prompt.shape
    b_dec, t_dec, _ = x_sample.shape
    depth = w_in.shape[0]
    past_len = page_table.shape[1] * cache_ckv.shape[2]
    n_pr, n_dec = b_pr * t_pr, b_dec * t_dec
    cos_p, sin_p = _rope_tables(jnp.arange(t_pr))
    cos_s, sin_s = _rope_tables(past_len + jnp.arange(n_dec) % t_dec)

    h_p = x_prompt.reshape(n_pr, d_model)
    h_s = x_sample.reshape(n_dec, d_model)
    outs = [[] for _ in range(6)]
    for l in range(depth):
        prm = _prep_layer(l, g_ff1, w1_gate, w1_up, w1_down, g_mix, w_in, g_q, w_q_up, g_kv, w_uk, w_uv, conv_w,
                          conv_b, g_cn, b_cn, w_o, g_ff2, w2_gate, w2_up, w2_down, g_ple, w_pg, b_pg, w_pp,
                          g_final)
        h1 = _ffn(h_p, prm["g_ff1"], prm["w1_gate"], prm["w1_up"], prm["w1_down"], TM_FFN)
        conv_zero = jnp.zeros((b_pr, CONV_WIDTH - 1, CONV_CH), F32)
        ckv, kpe, c, conv_state, q, k, v = _inproj(h1, prm, cos_p, sin_p, TM_PROJ, prompt_bt=(b_pr, t_pr),
                                                   hist=conv_zero)
        att = _flash(q, k, v, TQ_FLASH).reshape(n_pr, -1)
        h_p = _post(h1, att, c, p_prompt[l].reshape(n_pr, -1), prm, TM_POST,
                    l == depth - 1)
        outs[0].append(ckv.reshape(b_pr, t_pr, KV_RANK))
        outs[1].append(kpe.reshape(b_pr, t_pr, QK_ROPE))
        outs[2].append(conv_state)

        h1 = _ffn(h_s, prm["g_ff1"], prm["w1_gate"], prm["w1_up"], prm["w1_down"], n_dec)
        ckv, kpe, u, q_abs, q_pe = _inproj(h1, prm, cos_s, sin_s, n_dec)
        rows = N_HEADS * t_dec
        per_req = lambda a: jnp.transpose(a.reshape(N_HEADS, b_dec, t_dec, a.shape[-1]), (1, 0, 2, 3)
                                          ).reshape(b_dec, rows, a.shape[-1])
        ckv_new = ckv.reshape(b_dec, t_dec, KV_RANK)
        kpe_new = kpe.reshape(b_dec, t_dec, QK_ROPE)
        o_lat = _sattn(page_table, per_req(q_abs), per_req(q_pe[..., :QK_ROPE]), ckv_new, kpe_new,
                       cache_ckv[l], jnp.swapaxes(cache_kpe[l], 1, 2), SAMPLE_PAGES_PER_STEP)
        o_lat = jnp.transpose(o_lat.reshape(b_dec, N_HEADS, t_dec, KV_RANK), (1, 0, 2, 3)
                              ).reshape(N_HEADS, n_dec, KV_RANK)
        att = _uv(o_lat, prm["w_uv_pad"])
        c, conv_state = _conv(u.reshape(b_dec, t_dec, CONV_CH), state_conv[l], prm["conv_w"], prm["conv_b"],
                              prm["g_cn"], prm["b_cn"], SAMPLE_CONV_BATCH, t_dec)
        h_s = _post(h1, att, c.reshape(n_dec, CONV_CH), p_sample[l].reshape(n_dec, -1), prm, TM_POST,
                    l == depth - 1)
        outs[3].append(ckv_new)
        outs[4].append(kpe_new)
        outs[5].append(conv_state)

    y_prompt = h_p.reshape(b_pr, t_pr, d_model)
    y_sample = h_s.reshape(b_dec, t_dec, d_model)
    return (y_prompt, y_sample, jnp.stack(outs[0]), jnp.stack(outs[1]), jnp.stack(outs[2]),
            jnp.stack(outs[3]), jnp.stack(outs[4]), jnp.stack(outs[5]))
```

```python
import functools

import jax
import jax.numpy as jnp
from jax import lax
from jax.experimental import pallas as pl
from jax.experimental.pallas import tpu as pltpu

F32 = jnp.float32
BF16 = jnp.bfloat16

N_HEADS = 8
QK_NOPE = 64
QK_ROPE = 32
V_HEAD = 64
Q_RANK = 384
KV_RANK = 256
CONV_CH = 512
CONV_WIDTH = 31
ROPE_BASE = 10000.0
EPS = 1e-6
SCALE = (QK_NOPE + QK_ROPE) ** -0.5
LOG2_E = 1.4426950408889634
Q_SCALE = SCALE * LOG2_E
NEG_INF = -1e30

LANES = 128
SUBLANES = 8
HEAD_PAD = LANES
CONV_HALO = 32
VMEM_LIMIT = 56 * 1024 * 1024


def _rms(x, g):
    return x * lax.rsqrt(jnp.mean(x * x, axis=-1, keepdims=True) + EPS) * g


def _dot(a, b):
    return jnp.dot(a, b, preferred_element_type=F32)


def _dot_t(a, b):
    return lax.dot_general(a, b, (((1,), (1,)), ((), ())), preferred_element_type=F32)


def _resident(shape):
    nd = len(shape)
    return pl.BlockSpec(shape, lambda *_: (0,) * nd, pipeline_mode=pl.Buffered(1))


def _params(n_axes, flags=None):
    return pltpu.CompilerParams(dimension_semantics=("arbitrary",) * n_axes,
                                vmem_limit_bytes=VMEM_LIMIT, flags=flags)


def _swiglu(n, wg_ref, wu_ref, wd_ref):
    gate = _dot(n, wg_ref[...])
    up = _dot(n, wu_ref[...])
    a = (gate * jax.nn.sigmoid(gate) * up).astype(BF16)
    return _dot(a, wd_ref[...])


def _ffn_body(x_ref, g_ref, wg_ref, wu_ref, wd_ref, o_ref):
    x = x_ref[...]
    n = _rms(x, g_ref[...]).astype(BF16)
    o_ref[...] = x + 0.5 * _swiglu(n, wg_ref, wu_ref, wd_ref)


def _ffn(x, g, wg, wu, wd, tm):
    n, d = x.shape
    row = pl.BlockSpec((tm, d), lambda i: (i, 0))
    return pl.pallas_call(
        _ffn_body,
        grid=(n // tm,),
        in_specs=[row, _resident(g.shape), _resident(wg.shape), _resident(wu.shape), _resident(wd.shape)],
        out_specs=row,
        out_shape=jax.ShapeDtypeStruct((n, d), F32),
        compiler_params=_params(1),
        name="ffn1",
    )(x, g, wg, wu, wd)


_O_CQ = 0
_O_CKV = _O_CQ + Q_RANK
_O_UA = _O_CKV + KV_RANK
_O_UB = _O_UA + CONV_CH
_O_KPE = _O_UB + CONV_CH
_O_KPES = _O_KPE + LANES
_W_IN_COLS = _O_KPES + LANES


def _inproj_common(h_ref, gmix_ref, win_ref, gq_ref, wq_ref, wqs_ref, gkv_ref, cos_ref, sin_ref,
                   ckv_ref, kpe_ref):
    n = _rms(h_ref[...], gmix_ref[...]).astype(BF16)
    z = _dot(n, win_ref[...])
    cos = cos_ref[...]
    sin = sin_ref[...]
    cqn = _rms(z[:, _O_CQ:_O_CKV], gq_ref[...]).astype(BF16)
    q = _dot(cqn, wq_ref[...])
    qs = _dot(cqn, wqs_ref[...])
    ckv = _rms(z[:, _O_CKV:_O_UA], gkv_ref[...])
    kpe = z[:, _O_KPE:_O_KPES] * cos + z[:, _O_KPES:_W_IN_COLS] * sin
    ckv_ref[...] = ckv
    kpe_ref[...] = kpe[:, :QK_ROPE]
    u = z[:, _O_UA:_O_UB] * jax.nn.sigmoid(z[:, _O_UB:_O_KPE])
    q_heads = []
    for h in range(N_HEADS):
        hs = slice(h * HEAD_PAD, (h + 1) * HEAD_PAD)
        q_heads.append((q[:, hs] * cos + qs[:, hs] * sin) * Q_SCALE)
    return q_heads, ckv, kpe, u


def _inproj_prompt_body(tpb, h_ref, gmix_ref, win_ref, gq_ref, wq_ref, wqs_ref, gkv_ref, cos_ref, sin_ref,
                        wuk_ref, wuv_ref, hist_ref, cw_ref, cb_ref, gcn_ref, bcn_ref,
                        ckv_ref, kpe_ref, c_ref, state_ref, q_ref, k_ref, v_ref, ext_ref, rot_ref):
    _conv_init(pl.program_id(0) == 0, h_ref.shape[0], ext_ref)
    q_heads, ckv, kpe, u = _inproj_common(h_ref, gmix_ref, win_ref, gq_ref, wq_ref, wqs_ref, gkv_ref,
                                          cos_ref, sin_ref, ckv_ref, kpe_ref)
    _conv_tile(u.shape[0], pl.program_id(0) % tpb == 0, u[None], hist_ref, cw_ref, cb_ref, gcn_ref, bcn_ref,
               c_ref, state_ref, ext_ref, rot_ref)
    ckvb = ckv.astype(BF16)
    kn = _dot(ckvb, wuk_ref[...])
    vv = _dot(ckvb, wuv_ref[...])
    for h in range(N_HEADS):
        hs = slice(h * HEAD_PAD, (h + 1) * HEAD_PAD)
        q_ref[0, h] = q_heads[h].astype(BF16)
        k_ref[0, h] = (kn[:, hs] + kpe).astype(BF16)
    lane = lax.broadcasted_iota(jnp.int32, (vv.shape[0], LANES), 1)
    for hp in range(N_HEADS // 2):
        pair = vv[:, hp * LANES:(hp + 1) * LANES]
        v_ref[0, 2 * hp] = jnp.where(lane < V_HEAD, pair, (lane == V_HEAD).astype(F32)).astype(BF16)
        v_ref[0, 2 * hp + 1] = jnp.where(lane >= V_HEAD, pair, (lane == 0).astype(F32)).astype(BF16)


def _inproj_sample_body(h_ref, gmix_ref, win_ref, gq_ref, wq_ref, wqs_ref, gkv_ref, cos_ref, sin_ref,
                        wukt_ref,
                        ckv_ref, kpe_ref, u_ref, qabs_ref, qpe_ref):
    q_heads, _, _, u = _inproj_common(h_ref, gmix_ref, win_ref, gq_ref, wq_ref, wqs_ref, gkv_ref,
                                      cos_ref, sin_ref, ckv_ref, kpe_ref)
    u_ref[...] = u
    for h in range(N_HEADS):
        qh = q_heads[h].astype(BF16)
        qpe_ref[h] = qh
        qabs_ref[h] = _dot(qh, wukt_ref[h]).astype(BF16)


def _inproj(h, prm, cos, sin, tm, prompt_bt=None, hist=None):
    n, d = h.shape
    tab_tiles = cos.shape[0] // tm
    row = lambda w: pl.BlockSpec((tm, w), lambda i: (i, 0))
    tab = pl.BlockSpec((tm, LANES), lambda i: (i % tab_tiles, 0))
    common_in = [row(d), _resident(prm["g_mix"].shape), _resident(prm["w_in"].shape),
                 _resident(prm["g_q"].shape), _resident(prm["w_q"].shape), _resident(prm["w_qs"].shape),
                 _resident(prm["g_kv"].shape), tab, tab]
    common_args = (h, prm["g_mix"], prm["w_in"], prm["g_q"], prm["w_q"], prm["w_qs"], prm["g_kv"], cos, sin)
    common_out = [row(KV_RANK), row(QK_ROPE)]
    common_shape = [jax.ShapeDtypeStruct((n, KV_RANK), F32), jax.ShapeDtypeStruct((n, QK_ROPE), F32)]
    if prompt_bt is not None:
        b, t = prompt_bt
        tpb = t // tm
        head = lambda nh: pl.BlockSpec((1, nh, tm, LANES), lambda i: (i // tpb, 0, i % tpb, 0))
        conv_names = ["conv_w", "conv_b", "g_cn", "b_cn"]
        per_seq = lambda rows: pl.BlockSpec((1, rows, CONV_CH), lambda i: (i // tpb, 0, 0))
        return pl.pallas_call(
            functools.partial(_inproj_prompt_body, tpb),
            grid=(n // tm,),
            in_specs=common_in + [_resident(prm["w_uk"].shape), _resident(prm["w_uv"].shape), per_seq(CONV_HALO)]
                     + [_resident(prm[k].shape) for k in conv_names],
            out_specs=common_out + [row(CONV_CH), per_seq(CONV_WIDTH - 1)] + [head(N_HEADS)] * 3,
            out_shape=common_shape + [jax.ShapeDtypeStruct((n, CONV_CH), BF16),
                                      jax.ShapeDtypeStruct((b, CONV_WIDTH - 1, CONV_CH), F32)]
                      + [jax.ShapeDtypeStruct((b, N_HEADS, t, LANES), BF16)] * 3,
            scratch_shapes=_conv_scratch(1, tm, CONV_CH),
            compiler_params=_params(1),
            name="inproj_prompt",
        )(*common_args, prm["w_uk"], prm["w_uv"], _pad_hist(hist), *[prm[k] for k in conv_names])
    hrow = lambda w: pl.BlockSpec((N_HEADS, tm, w), lambda i: (0, i, 0))
    return pl.pallas_call(
        _inproj_sample_body,
        grid=(n // tm,),
        in_specs=common_in + [_resident(prm["w_ukt"].shape)],
        out_specs=common_out + [row(CONV_CH), hrow(KV_RANK), hrow(LANES)],
        out_shape=common_shape + [jax.ShapeDtypeStruct((n, CONV_CH), F32),
                                  jax.ShapeDtypeStruct((N_HEADS, n, KV_RANK), BF16),
                                  jax.ShapeDtypeStruct((N_HEADS, n, LANES), BF16)],
        compiler_params=_params(1),
        name="inproj_sample",
    )(*common_args, prm["w_ukt"])


def _conv_init(first_step, tt, ext_ref):
    @pl.when(first_step)
    def _():
        ext_ref[:, tt:tt + CONV_HALO, :] = jnp.zeros((ext_ref.shape[0], CONV_HALO, ext_ref.shape[2]), F32)


def _conv_tile(tt, first, v, hist_ref, w_ref, b_ref, g_ref, beta_ref, c_ref, state_ref, ext_ref, rot_ref):
    ext_ref[:, 0:CONV_HALO, :] = jnp.where(first, hist_ref[...], ext_ref[:, tt:tt + CONV_HALO, :])
    ext_ref[:, CONV_HALO:CONV_HALO + tt, :] = v
    off = CONV_HALO - (CONV_WIDTH - 1)
    span = rot_ref.shape[2]
    for r in range(1, SUBLANES):
        rot_ref[r - 1] = ext_ref[:, r:r + span, :]

    def tap(k):
        r = (off + k) % SUBLANES
        base = off + k - r
        win = ext_ref[:, base:base + tt, :] if r == 0 else rot_ref[r - 1, :, base:base + tt, :]
        return win * w_ref[k:k + 1, :]

    acc = tap(0)
    for k in range(1, CONV_WIDTH):
        acc = acc + tap(k)
    acc = acc + b_ref[...]
    mu = jnp.mean(acc, axis=-1, keepdims=True)
    cen = acc - mu
    var = jnp.mean(cen * cen, axis=-1, keepdims=True)
    y = cen * lax.rsqrt(var + EPS) * g_ref[...] + beta_ref[...]
    c_ref[...] = (y * jax.nn.sigmoid(y)).astype(BF16).reshape(c_ref.shape)
    state_ref[...] = ext_ref[:, tt + off:tt + CONV_HALO, :]


def _conv_scratch(nb, tt, ch):
    return [pltpu.VMEM((nb, tt + CONV_HALO, ch), F32),
            pltpu.VMEM((SUBLANES - 1, nb, tt + CONV_HALO - SUBLANES, ch), F32)]


def _pad_hist(hist):
    return jnp.pad(hist, ((0, 0), (CONV_HALO - hist.shape[1], 0), (0, 0)))


def _conv_body(tt, v_ref, hist_ref, w_ref, b_ref, g_ref, beta_ref, c_ref, state_ref, ext_ref, rot_ref):
    ti = pl.program_id(1)
    _conv_init(jnp.logical_and(pl.program_id(0) == 0, ti == 0), tt, ext_ref)
    _conv_tile(tt, ti == 0, v_ref[...], hist_ref, w_ref, b_ref, g_ref, beta_ref, c_ref, state_ref, ext_ref, rot_ref)


def _conv(v, hist, w, b, g, beta, nb, tt):
    bsz, t, ch = v.shape
    hist = _pad_hist(hist)
    return pl.pallas_call(
        functools.partial(_conv_body, tt),
        grid=(bsz // nb, t // tt),
        in_specs=[pl.BlockSpec((nb, tt, ch), lambda i, j: (i, j, 0)),
                  pl.BlockSpec((nb, CONV_HALO, ch), lambda i, j: (i, 0, 0)),
                  _resident(w.shape), _resident(b.shape), _resident(g.shape), _resident(beta.shape)],
        out_specs=[pl.BlockSpec((nb, tt, ch), lambda i, j: (i, j, 0)),
                   pl.BlockSpec((nb, CONV_WIDTH - 1, ch), lambda i, j: (i, 0, 0))],
        out_shape=[jax.ShapeDtypeStruct((bsz, t, ch), BF16),
                   jax.ShapeDtypeStruct((bsz, CONV_WIDTH - 1, ch), F32)],
        scratch_shapes=_conv_scratch(nb, tt, ch),
        compiler_params=_params(2),
        name="conv",
    )(v, hist, w, b, g, beta)


def _flash_body(tq, unroll, q_ref, k_ref, v_ref, o_ref):
    qi = pl.program_id(2)
    qs = (q_ref[0, 0], q_ref[0, 1])

    def absorb(j, state, masked):
        start = pl.multiple_of(j * tq, tq)
        out = []
        for hh in range(2):
            m_prev, acc = state[hh]
            s = _dot_t(qs[hh], k_ref[0, hh, pl.ds(start, tq), :])
            if masked:
                row = lax.broadcasted_iota(jnp.int32, (tq, tq), 0)
                col = lax.broadcasted_iota(jnp.int32, (tq, tq), 1)
                s = jnp.where(row >= col, s, NEG_INF)
            m_new = jnp.maximum(m_prev, jnp.max(s, axis=-1, keepdims=True))
            p = jnp.exp2((s - m_new).astype(BF16))
            pv = _dot(p, v_ref[0, hh, pl.ds(start, tq), :])
            out.append((m_new, jnp.exp2(m_prev - m_new) * acc + pv))
        return tuple(out)

    def trip(g, state):
        for u in range(unroll):
            state = absorb(g * unroll + u, state, False)
        return state

    init_h = (jnp.full((tq, 1), NEG_INF, F32), jnp.zeros((tq, LANES), F32))
    state = lax.fori_loop(0, qi // unroll, trip, (init_h, init_h))
    u = unroll // 2
    while u >= 1:
        base = (qi // (2 * u)) * (2 * u)

        def leftover(st, base=base, u=u):
            for i in range(u):
                st = absorb(base + i, st, False)
            return st

        state = lax.cond((qi // u) % 2 == 1, leftover, lambda st: st, state)
        u //= 2
    (_, acc0), (_, acc1) = absorb(qi, state, True)
    lane = lax.broadcasted_iota(jnp.int32, (tq, LANES), 1)
    o_ref[0] = jnp.where(lane < V_HEAD, acc0 / acc0[:, V_HEAD:V_HEAD + 1], acc1 / acc1[:, 0:1]).astype(BF16)


def _flash(q, k, v, tq):
    b, nh, t, _ = q.shape
    return pl.pallas_call(
        functools.partial(_flash_body, tq, FLASH_UNROLL),
        grid=(b, nh // 2, t // tq),
        in_specs=[pl.BlockSpec((1, 2, tq, LANES), lambda bi, hp, qi: (bi, hp, qi, 0)),
                  pl.BlockSpec((1, 2, t, LANES), lambda bi, hp, qi: (bi, hp, 0, 0)),
                  pl.BlockSpec((1, 2, t, LANES), lambda bi, hp, qi: (bi, hp, 0, 0))],
        out_specs=pl.BlockSpec((1, tq, LANES), lambda bi, hp, qi: (bi, qi, hp)),
        out_shape=jax.ShapeDtypeStruct((b, t, nh * V_HEAD), BF16),
        compiler_params=_params(3),
        name="flash_prompt",
    )(q, k, v)


def _sattn_body(ch, sub, t_new, pt_ref, q_ref, qpe_ref, cnew_ref, knew_ref, ckv_hbm, kpe_hbm, o_ref,
                cbuf, kbuf, sem, m_ref, l_ref, acc_ref):
    b = pl.program_id(0)
    c = pl.program_id(1)
    nc = pl.num_programs(1)
    total = pl.num_programs(0) * nc
    step = b * nc + c
    slot = step % 2
    page = cbuf.shape[1] // ch

    def page_copies(bb, cc, sl):
        for j in range(ch):
            pg = pt_ref[bb, cc * ch + j]
            dst = pl.ds(j * page, page)
            yield pltpu.make_async_copy(ckv_hbm.at[pg], cbuf.at[sl, dst, :], sem.at[0, sl])
            yield pltpu.make_async_copy(kpe_hbm.at[pg], kbuf.at[sl, j], sem.at[1, sl])

    @pl.when(step == 0)
    def _():
        for cp in page_copies(0, 0, 0):
            cp.start()

    @pl.when(step + 1 < total)
    def _():
        nxt = step + 1
        for cp in page_copies(nxt // nc, nxt % nc, 1 - slot):
            cp.start()

    @pl.when(c == 0)
    def _():
        m_ref[...] = jnp.full(m_ref.shape, NEG_INF, F32)
        l_ref[...] = jnp.zeros(l_ref.shape, F32)
        acc_ref[...] = jnp.zeros(acc_ref.shape, F32)

    for cp in page_copies(b, c, slot):
        cp.wait()

    q = q_ref[0]
    qpe = qpe_ref[0]
    m_run, l_run, acc = m_ref[...], l_ref[...], acc_ref[...]
    for g in range(ch // sub):
        ck = cbuf[slot, g * sub * page:(g + 1) * sub * page, :].astype(BF16)
        s_pe = jnp.concatenate([_dot(qpe, kbuf[slot, g * sub + j].astype(BF16)) for j in range(sub)], axis=1)
        s = _dot_t(q, ck) + s_pe
        m_new = jnp.maximum(m_run, jnp.max(s, axis=-1, keepdims=True))
        alpha = jnp.exp2(m_run - m_new)
        p = jnp.exp2(s - m_new)
        l_run = alpha * l_run + jnp.sum(p, axis=-1, keepdims=True)
        acc = alpha * acc + _dot(p.astype(BF16), ck)
        m_run = m_new
    l_ref[...] = l_run
    acc_ref[...] = acc
    m_ref[...] = m_run

    @pl.when(c == nc - 1)
    def _():
        qf = q.astype(F32)
        qpf = qpe.astype(F32)
        cn = cnew_ref[0].astype(BF16).astype(F32)
        kn = knew_ref[0].astype(BF16).astype(F32)
        t_row = lax.broadcasted_iota(jnp.int32, (q.shape[0], 1), 0) % t_new
        s_new = []
        for j in range(t_new):
            sj = (jnp.sum(qf * cn[j:j + 1, :], axis=-1, keepdims=True)
                  + jnp.sum(qpf * kn[j:j + 1, :], axis=-1, keepdims=True))
            s_new.append(jnp.where(t_row >= j, sj, NEG_INF))
        m_old = m_ref[...]
        m_fin = m_old
        for sj in s_new:
            m_fin = jnp.maximum(m_fin, sj)
        a_fin = jnp.exp2(m_old - m_fin)
        l_fin = a_fin * l_ref[...]
        acc = a_fin * acc_ref[...]
        for j, sj in enumerate(s_new):
            pj = jnp.exp2(sj - m_fin)
            l_fin = l_fin + pj
            acc = acc + pj.astype(BF16).astype(F32) * cn[j:j + 1, :]
        o_ref[0] = acc / l_fin


def _sattn(page_table, q, qpe, ckv_new, kpe_new, cache_ckv, cache_kpe, ch):
    bsz, rows, _ = q.shape
    n_pages = page_table.shape[1]
    page = cache_ckv.shape[1]
    t_new = ckv_new.shape[1]
    grid_spec = pltpu.PrefetchScalarGridSpec(
        num_scalar_prefetch=1,
        grid=(bsz, n_pages // ch),
        in_specs=[pl.BlockSpec((1, rows, KV_RANK), lambda b, c, pt: (b, 0, 0)),
                  pl.BlockSpec((1, rows, QK_ROPE), lambda b, c, pt: (b, 0, 0)),
                  pl.BlockSpec((1, t_new, KV_RANK), lambda b, c, pt: (b, 0, 0)),
                  pl.BlockSpec((1, t_new, QK_ROPE), lambda b, c, pt: (b, 0, 0)),
                  pl.BlockSpec(memory_space=pl.ANY),
                  pl.BlockSpec(memory_space=pl.ANY)],
        out_specs=pl.BlockSpec((1, rows, KV_RANK), lambda b, c, pt: (b, 0, 0)),
        scratch_shapes=[pltpu.VMEM((2, ch * page, KV_RANK), F32),
                        pltpu.VMEM((2, ch, QK_ROPE, page), F32),
                        pltpu.SemaphoreType.DMA((2, 2)),
                        pltpu.VMEM((rows, 1), F32),
                        pltpu.VMEM((rows, 1), F32),
                        pltpu.VMEM((rows, KV_RANK), F32)],
    )
    return pl.pallas_call(
        functools.partial(_sattn_body, ch, SAMPLE_PAGES_PER_SOFTMAX, t_new),
        grid_spec=grid_spec,
        out_shape=jax.ShapeDtypeStruct((bsz, rows, KV_RANK), F32),
        compiler_params=_params(2),
        name="attn_sample",
    )(page_table, q, qpe, ckv_new, kpe_new, cache_ckv, cache_kpe)


def _uv_body(o_ref, w_ref, att_ref):
    acc = _dot(o_ref[0].astype(BF16), w_ref[0])
    for h in range(1, N_HEADS):
        acc = acc + _dot(o_ref[h].astype(BF16), w_ref[h])
    att_ref[...] = acc.astype(BF16)


def _uv(o_lat, w_uv_pad):
    _, n, _ = o_lat.shape
    return pl.pallas_call(
        _uv_body,
        grid=(1,),
        in_specs=[_resident(o_lat.shape), _resident(w_uv_pad.shape)],
        out_specs=pl.BlockSpec((n, N_HEADS * V_HEAD), lambda i: (0, 0)),
        out_shape=jax.ShapeDtypeStruct((n, N_HEADS * V_HEAD), BF16),
        compiler_params=_params(1),
        name="uv_sample",
    )(o_lat, w_uv_pad)


def _post_body(last_layer, h_ref, att_ref, c_ref, p_ref, woa_ref, woc_ref, g2_ref, wg_ref, wu_ref, wd_ref,
               gple_ref, wpg_ref, bpg_ref, wpp_ref, gfin_ref, y_ref):
    h = h_ref[...] + _dot(att_ref[...], woa_ref[...]) + _dot(c_ref[...], woc_ref[...])
    h = h + 0.5 * _swiglu(_rms(h, g2_ref[...]).astype(BF16), wg_ref, wu_ref, wd_ref)
    gate = jax.nn.sigmoid(_dot(_rms(h, gple_ref[...]).astype(BF16), wpg_ref[...]) + bpg_ref[...])
    h = h + gate * _dot(p_ref[...].astype(BF16), wpp_ref[...])
    y_ref[...] = _rms(h, gfin_ref[...]) if last_layer else h


def _post(h, att, c, p, prm, tm, last_layer):
    n, d = h.shape
    row = lambda w: pl.BlockSpec((tm, w), lambda i: (i, 0))
    names = ["w_o_att", "w_o_conv", "g_ff2", "w2_gate", "w2_up", "w2_down", "g_ple", "w_pg", "b_pg", "w_pp",
             "g_final"]
    return pl.pallas_call(
        functools.partial(_post_body, last_layer),
        grid=(n // tm,),
        in_specs=[row(d), row(att.shape[1]), row(c.shape[1]), row(p.shape[1])]
                 + [_resident(prm[k].shape) for k in names],
        out_specs=row(d),
        out_shape=jax.ShapeDtypeStruct((n, d), F32),
        compiler_params=_params(1),
        name="post",
    )(h, att, c, p, *[prm[k] for k in names])


def _rope_tables(pos):
    half = QK_ROPE // 2
    inv = ROPE_BASE ** (-jnp.arange(half, dtype=F32) / half)
    ang = pos.astype(F32)[:, None] * inv[None, :]
    cos, sin = jnp.cos(ang), jnp.sin(ang)
    n = pos.shape[0]
    cos_t = jnp.concatenate([cos, cos, jnp.ones((n, QK_NOPE), F32), jnp.zeros((n, HEAD_PAD - QK_ROPE - QK_NOPE), F32)],
                            axis=1)
    sin_t = jnp.concatenate([-sin, sin, jnp.zeros((n, HEAD_PAD - QK_ROPE), F32)], axis=1)
    return cos_t, sin_t


def _swap_halves(w):
    half = QK_ROPE // 2
    return jnp.concatenate([w[..., half:], w[..., :half]], axis=-1)


def _prep_layer(l, g_ff1, w1_gate, w1_up, w1_down, g_mix, w_in, g_q, w_q_up, g_kv, w_uk, w_uv, conv_w, conv_b,
                g_cn, b_cn, w_o, g_ff2, w2_gate, w2_up, w2_down, g_ple, w_pg, b_pg, w_pp, g_final):
    vec = lambda a: a.reshape(1, -1)
    d_model = w_in.shape[1]
    prm = dict(g_ff1=vec(g_ff1[l]), w1_gate=w1_gate[l].astype(BF16), w1_up=w1_up[l].astype(BF16),
               w1_down=w1_down[l].astype(BF16), g_mix=vec(g_mix[l]), g_q=vec(g_q[l]), g_kv=vec(g_kv[l]),
               conv_w=conv_w[l], conv_b=vec(conv_b[l]), g_cn=vec(g_cn[l]), b_cn=vec(b_cn[l]),
               g_ff2=vec(g_ff2[l]), w2_gate=w2_gate[l].astype(BF16), w2_up=w2_up[l].astype(BF16),
               w2_down=w2_down[l].astype(BF16), g_ple=vec(g_ple[l]), w_pg=w_pg[l].astype(BF16),
               b_pg=vec(b_pg[l]), w_pp=w_pp[l].astype(BF16), g_final=vec(g_final))
    wi = w_in[l]
    o1, o2, o3 = Q_RANK, Q_RANK + KV_RANK, Q_RANK + KV_RANK + QK_ROPE
    w_kpe = wi[:, o2:o3]
    lane_pad = jnp.zeros((d_model, LANES - QK_ROPE), F32)
    prm["w_in"] = jnp.concatenate(
        [wi[:, :o2], wi[:, o3:], w_kpe, lane_pad, _swap_halves(w_kpe), lane_pad], axis=1).astype(BF16)
    wq = w_q_up[l]
    wq_nope, wq_pe = wq[..., :QK_NOPE], wq[..., QK_NOPE:]
    zq = lambda w: jnp.zeros(wq.shape[:2] + (w,), F32)
    prm["w_q"] = jnp.concatenate([wq_pe, wq_nope, zq(HEAD_PAD - QK_ROPE - QK_NOPE)], axis=-1
                                 ).reshape(Q_RANK, N_HEADS * HEAD_PAD).astype(BF16)
    prm["w_qs"] = jnp.concatenate([_swap_halves(wq_pe), zq(HEAD_PAD - QK_ROPE)], axis=-1
                                  ).reshape(Q_RANK, N_HEADS * HEAD_PAD).astype(BF16)
    wuk = w_uk[l]
    zk = lambda w: jnp.zeros(wuk.shape[:2] + (w,), F32)
    wuk_pad = jnp.concatenate([zk(QK_ROPE), wuk, zk(HEAD_PAD - QK_ROPE - QK_NOPE)], axis=-1)
    prm["w_uk"] = wuk_pad.reshape(KV_RANK, N_HEADS * HEAD_PAD).astype(BF16)
    prm["w_ukt"] = jnp.transpose(wuk_pad, (1, 2, 0)).astype(BF16)
    wuv = w_uv[l]
    prm["w_uv"] = wuv.reshape(KV_RANK, N_HEADS * V_HEAD).astype(BF16)
    eye = jnp.eye(N_HEADS, dtype=F32)
    prm["w_uv_pad"] = jnp.einsum("rhv,hg->hrgv", wuv, eye).reshape(N_HEADS, KV_RANK, N_HEADS * V_HEAD).astype(BF16)
    d_att = N_HEADS * V_HEAD
    prm["w_o_att"] = w_o[l][:d_att].astype(BF16)
    prm["w_o_conv"] = w_o[l][d_att:].astype(BF16)
    return prm


TM_FFN = 512
TM_PROJ = 512
TM_POST = 512
TQ_FLASH = 512
FLASH_UNROLL = 4
SAMPLE_CONV_BATCH = 16
SAMPLE_PAGES_PER_STEP = 128
SAMPLE_PAGES_PER_SOFTMAX = 32


def kernel(x_prompt, x_sample, cache_ckv, cache_kpe, state_conv, page_table, p_prompt, p_sample, g_ff1, w1_gate,
           w1_up, w1_down, g_mix, w_in, g_q, w_q_up, g_kv, w_uk, w_uv, conv_w, conv_b, g_cn, b_cn, w_o, g_ff2,
           w2_gate, w2_up, w2_down, g_ple, w_pg, b_pg, w_pp, g_final):
    b_pr, t_pr, d_model = x_prompt.shape
    b_dec, t_dec, _ = x_sample.shape
    depth = w_in.shape[0]
    past_len = page_table.shape[1] * cache_ckv.shape[2]
    n_pr, n_dec = b_pr * t_pr, b_dec * t_dec
    cos_p, sin_p = _rope_tables(jnp.arange(t_pr))
    cos_s, sin_s = _rope_tables(past_len + jnp.arange(n_dec) % t_dec)

    h_p = x_prompt.reshape(n_pr, d_model)
    h_s = x_sample.reshape(n_dec, d_model)
    outs = [[] for _ in range(6)]
    for l in range(depth):
        prm = _prep_layer(l, g_ff1, w1_gate, w1_up, w1_down, g_mix, w_in, g_q, w_q_up, g_kv, w_uk, w_uv, conv_w,
                          conv_b, g_cn, b_cn, w_o, g_ff2, w2_gate, w2_up, w2_down, g_ple, w_pg, b_pg, w_pp,
                          g_final)
        h1 = _ffn(h_p, prm["g_ff1"], prm["w1_gate"], prm["w1_up"], prm["w1_down"], TM_FFN)
        conv_zero = jnp.zeros((b_pr, CONV_WIDTH - 1, CONV_CH), F32)
        ckv, kpe, c, conv_state, q, k, v = _inproj(h1, prm, cos_p, sin_p, TM_PROJ, prompt_bt=(b_pr, t_pr),
                                                   hist=conv_zero)
        att = _flash(q, k, v, TQ_FLASH).reshape(n_pr, -1)
        h_p = _post(h1, att, c, p_prompt[l].reshape(n_pr, -1), prm, TM_POST,
                    l == depth - 1)
        outs[0].append(ckv.reshape(b_pr, t_pr, KV_RANK))
        outs[1].append(kpe.reshape(b_pr, t_pr, QK_ROPE))
        outs[2].append(conv_state)

        h1 = _ffn(h_s, prm["g_ff1"], prm["w1_gate"], prm["w1_up"], prm["w1_down"], n_dec)
        ckv, kpe, u, q_abs, q_pe = _inproj(h1, prm, cos_s, sin_s, n_dec)
        rows = N_HEADS * t_dec
        per_req = lambda a: jnp.transpose(a.reshape(N_HEADS, b_dec, t_dec, a.shape[-1]), (1, 0, 2, 3)
                                          ).reshape(b_dec, rows, a.shape[-1])
        ckv_new = ckv.reshape(b_dec, t_dec, KV_RANK)
        kpe_new = kpe.reshape(b_dec, t_dec, QK_ROPE)
        o_lat = _sattn(page_table, per_req(q_abs), per_req(q_pe[..., :QK_ROPE]), ckv_new, kpe_new,
                       cache_ckv[l], jnp.swapaxes(cache_kpe[l], 1, 2), SAMPLE_PAGES_PER_STEP)
        o_lat = jnp.transpose(o_lat.reshape(b_dec, N_HEADS, t_dec, KV_RANK), (1, 0, 2, 3)
                              ).reshape(N_HEADS, n_dec, KV_RANK)
        att = _uv(o_lat, prm["w_uv_pad"])
        c, conv_state = _conv(u.reshape(b_dec, t_dec, CONV_CH), state_conv[l], prm["conv_w"], prm["conv_b"],
                              prm["g_cn"], prm["b_cn"], SAMPLE_CONV_BATCH, t_dec)
        h_s = _post(h1, att, c.reshape(n_dec, CONV_CH), p_sample[l].reshape(n_dec, -1), prm, TM_POST,
                    l == depth - 1)
        outs[3].append(ckv_new)
        outs[4].append(kpe_new)
        outs[5].append(conv_state)

    y_prompt = h_p.reshape(b_pr, t_pr, d_model)
    y_sample = h_s.reshape(b_dec, t_dec, d_model)
    return (y_prompt, y_sample, jnp.stack(outs[0]), jnp.stack(outs[1]), jnp.stack(outs[2]),
            jnp.stack(outs[3]), jnp.stack(outs[4]), jnp.stack(outs[5]))
```
